```python
import jax
import jax.numpy as jnp
from jax import lax
import numpy as np

D_MODEL = 1024
BATCH = 4
SEQ = 8192
DEPTH = 1

GRID_W = 64
MEM_LEN = 256
EPS = 1e-6
NEG_INF = -1e30

NA_HEADS = 8
NA_WIDTH = D_MODEL // 2
NA_HEAD_DIM = NA_WIDTH // NA_HEADS
NA_WIN_ROWS = 8
NA_WIN_COLS = 16
NA_QBLOCK_COLS = 16
NA_KBLOCK_COLS = NA_QBLOCK_COLS + NA_WIN_COLS

POOL_SIZES = (2, 4, 8, 16)
POOL_WIDTH = D_MODEL // 2
POOL_GROUP = POOL_WIDTH // len(POOL_SIZES)

XA_HEADS = 4
XA_WIDTH = D_MODEL // 2
XA_HEAD_DIM = XA_WIDTH // XA_HEADS

N_BRANCH = 3
IN_COLS = 3 * NA_WIDTH + POOL_WIDTH + XA_WIDTH + N_BRANCH * D_MODEL

PEER_HEADS = 8
PEER_KEYS = 128
PEER_EXPERTS = PEER_KEYS * PEER_KEYS
PEER_QDIM = 256
PEER_TOPK = 16
PEER_CHUNK = 64

kernel_name = 'hybrid_na_pool_memxattn_peer_encoder'


def rms_norm(x, g):
    xf = x.astype(jnp.float32)
    y = xf * lax.rsqrt(jnp.mean(xf * xf, axis=-1, keepdims=True) + EPS)
    return (y * g.astype(jnp.float32)).astype(x.dtype)


def neighbourhood_attention(q, k, v, rpb):
    B, S, H, dh = q.shape
    rows = S // GRID_W
    wr = min(NA_WIN_ROWS, rows)
    n_cb = GRID_W // NA_QBLOCK_COLS
    qg = q.reshape(B, rows, GRID_W, H, dh)
    kg = k.reshape(B, rows, GRID_W, H, dh)
    vg = v.reshape(B, rows, GRID_W, H, dh)
    qc = np.arange(GRID_W).reshape(n_cb, NA_QBLOCK_COLS)
    kb0 = np.clip(np.arange(n_cb) * NA_QBLOCK_COLS - NA_WIN_COLS // 2, 0, GRID_W - NA_KBLOCK_COLS)
    kc = kb0[:, None] + np.arange(NA_KBLOCK_COLS)[None, :]
    cs = np.clip(qc - NA_WIN_COLS // 2, 0, GRID_W - NA_WIN_COLS)
    col_ok = (kc[:, None, :] >= cs[:, :, None]) & (kc[:, None, :] < cs[:, :, None] + NA_WIN_COLS)
    dc_idx = np.clip(kc[:, None, :] - qc[:, :, None], -(NA_WIN_COLS - 1), NA_WIN_COLS - 1) + NA_WIN_COLS - 1
    mask = np.broadcast_to(col_ok[:, :, None, :], (n_cb, NA_QBLOCK_COLS, wr, NA_KBLOCK_COLS))
    mask = mask.reshape(n_cb, NA_QBLOCK_COLS, wr * NA_KBLOCK_COLS)
    scale = dh ** -0.5

    def row_block(r):
        rs = jnp.clip(r - wr // 2, 0, rows - wr)
        q_r = lax.dynamic_index_in_dim(qg, r, axis=1, keepdims=False)
        k_r = lax.dynamic_slice_in_dim(kg, rs, wr, axis=1)
        v_r = lax.dynamic_slice_in_dim(vg, rs, wr, axis=1)
        q_b = q_r.reshape(B, n_cb, NA_QBLOCK_COLS, H, dh)
        k_b = jnp.stack([k_r[:, :, int(s):int(s) + NA_KBLOCK_COLS] for s in kb0], axis=1)
        v_b = jnp.stack([v_r[:, :, int(s):int(s) + NA_KBLOCK_COLS] for s in kb0], axis=1)
        k_b = k_b.reshape(B, n_cb, wr * NA_KBLOCK_COLS, H, dh)
        v_b = v_b.reshape(B, n_cb, wr * NA_KBLOCK_COLS, H, dh)
        dr_idx = rs + jnp.arange(wr) - r + (NA_WIN_ROWS - 1)
        bias = rpb[:, dr_idx[None, None, :, None], dc_idx[:, :, None, :]]
        bias = bias.reshape(H, n_cb, NA_QBLOCK_COLS, wr * NA_KBLOCK_COLS).astype(jnp.float32)
        s = jnp.einsum('bjqhd,bjkhd->bhjqk', q_b, k_b).astype(jnp.float32) * scale + bias[None]
        s = jnp.where(mask[None, None], s, NEG_INF)
        p = jax.nn.softmax(s, axis=-1).astype(v.dtype)
        o = jnp.einsum('bhjqk,bjkhd->bjqhd', p, v_b)
        return o.reshape(B, GRID_W, H, dh)

    out = lax.map(row_block, jnp.arange(rows))
    return out.transpose(1, 0, 2, 3, 4).reshape(B, S, H * dh)


def multiscale_pool(p, w_group, scale):
    B, S, C = p.shape
    pf = p.astype(jnp.float32)
    csum = jnp.concatenate([jnp.zeros((B, 1, C), jnp.float32), jnp.cumsum(pf, axis=1)], axis=1)
    t = np.arange(S)
    outs = []
    for g, w in enumerate(POOL_SIZES):
        lo = np.clip(t - w // 2, 0, S)
        hi = np.clip(t + w // 2, 0, S)
        cnt = (hi - lo).astype(np.float32)[None, :, None]
        sl = slice(g * POOL_GROUP, (g + 1) * POOL_GROUP)
        cg = csum[..., sl]
        outs.append((cg[:, hi] - cg[:, lo]) / cnt - pf[..., sl])
    pooled = jnp.stack(outs, axis=2).astype(p.dtype)
    mixed = jnp.einsum('bsgc,gcd->bsgd', pooled, w_group).reshape(B, S, C)
    return mixed * scale


def memory_attention(q_x, mem_n, w_mem_kv, q_g, k_g):
    B, S, _ = q_x.shape
    M = mem_n.shape[1]
    q = rms_norm(q_x.reshape(B, S, XA_HEADS, XA_HEAD_DIM), q_g)
    kv = mem_n @ w_mem_kv
    k = rms_norm(kv[..., :XA_WIDTH].reshape(B, M, XA_HEADS, XA_HEAD_DIM), k_g)
    v = kv[..., XA_WIDTH:].reshape(B, M, XA_HEADS, XA_HEAD_DIM)
    s = jnp.einsum('bshd,bmhd->bhsm', q, k).astype(jnp.float32) * (XA_HEAD_DIM ** -0.5)
    p = jax.nn.softmax(s, axis=-1).astype(v.dtype)
    return jnp.einsum('bhsm,bmhd->bshd', p, v).reshape(B, S, XA_WIDTH)


def peer_ffn(hn, w_q, sub_keys, u, v):
    B, S, D = hn.shape
    nc = S // PEER_CHUNK
    hc = hn.reshape(B, nc, PEER_CHUNK, D).transpose(1, 0, 2, 3)

    def chunk(hb):
        T = hb.shape[1]
        q = (hb @ w_q).reshape(B, T, PEER_HEADS, 2, PEER_QDIM // 2)
        s = jnp.einsum('bthpd,hpnd->bthpn', q, sub_keys).astype(jnp.float32)
        s1, i1 = lax.top_k(s[..., 0, :], PEER_TOPK)
        s2, i2 = lax.top_k(s[..., 1, :], PEER_TOPK)
        cand = (s1[..., :, None] + s2[..., None, :]).reshape(B, T, PEER_HEADS, PEER_TOPK * PEER_TOPK)
        cidx = (i1[..., :, None] * PEER_KEYS + i2[..., None, :]).reshape(B, T, PEER_HEADS, PEER_TOPK * PEER_TOPK)
        top_s, pos = lax.top_k(cand, PEER_TOPK)
        eidx = jnp.take_along_axis(cidx, pos, axis=-1)
        g = jax.nn.softmax(top_s, axis=-1)
        u_sel = u[eidx]
        v_sel = v[eidx]
        a = jax.nn.gelu(jnp.einsum('btd,bthkd->bthk', hb, u_sel).astype(jnp.float32), approximate=False)
        return jnp.einsum('bthk,bthkd->btd', (g * a).astype(hb.dtype), v_sel)

    out = lax.map(chunk, hc)
    return out.transpose(1, 0, 2, 3).reshape(B, S, D)


def setup_inputs(seed: int = 0) -> dict:
    key = jax.random.key(seed)
    ks = jax.random.split(key, 24)
    f32 = jnp.float32
    L = DEPTH

    def nrm(k, shape, scale):
        return jax.random.normal(k, shape, f32) * scale

    def gain(k, shape):
        return 1.0 + 0.05 * jax.random.normal(k, shape, f32)

    return {
        'x': nrm(ks[0], (BATCH, SEQ, D_MODEL), 1.0),
        'mem': nrm(ks[1], (BATCH, MEM_LEN, D_MODEL), 1.0),
        'mix_norm_g': gain(ks[2], (L, D_MODEL)),
        'mem_norm_g': gain(ks[3], (L, D_MODEL)),
        'w_in': nrm(ks[4], (L, D_MODEL, IN_COLS), D_MODEL ** -0.5),
        'gate_b': nrm(ks[5], (L, N_BRANCH * D_MODEL), 0.1),
        'w_mem_kv': nrm(ks[6], (L, D_MODEL, 2 * XA_WIDTH), D_MODEL ** -0.5),
        'na_q_g': gain(ks[7], (L, NA_HEAD_DIM)),
        'na_k_g': gain(ks[8], (L, NA_HEAD_DIM)),
        'na_rpb': nrm(ks[9], (L, NA_HEADS, 2 * NA_WIN_ROWS - 1, 2 * NA_WIN_COLS - 1), 0.5),
        'pool_w': nrm(ks[10], (L, len(POOL_SIZES), POOL_GROUP, POOL_GROUP), POOL_GROUP ** -0.5),
        'pool_scale': gain(ks[11], (L, POOL_WIDTH)),
        'xa_q_g': gain(ks[12], (L, XA_HEAD_DIM)),
        'xa_k_g': gain(ks[13], (L, XA_HEAD_DIM)),
        'w_branch_na': nrm(ks[14], (L, NA_WIDTH, D_MODEL), NA_WIDTH ** -0.5),
        'w_branch_pool': nrm(ks[15], (L, POOL_WIDTH, D_MODEL), POOL_WIDTH ** -0.5),
        'w_branch_xa': nrm(ks[16], (L, XA_WIDTH, D_MODEL), XA_WIDTH ** -0.5),
        'w_out': nrm(ks[17], (L, D_MODEL, D_MODEL), D_MODEL ** -0.5),
        'ffn_norm_g': gain(ks[18], (L, D_MODEL)),
        'peer_w_q': nrm(ks[19], (L, D_MODEL, PEER_HEADS * PEER_QDIM), D_MODEL ** -0.5),
        'peer_sub_keys': nrm(ks[20], (L, PEER_HEADS, 2, PEER_KEYS, PEER_QDIM // 2), (PEER_QDIM // 2) ** -0.5),
        'peer_u': nrm(ks[21], (L, PEER_EXPERTS, D_MODEL), D_MODEL ** -0.5),
        'peer_v': nrm(ks[22], (L, PEER_EXPERTS, D_MODEL), 0.5),
    }


def reference(x, mem, mix_norm_g, mem_norm_g, w_in, gate_b, w_mem_kv, na_q_g, na_k_g, na_rpb,
              pool_w, pool_scale, xa_q_g, xa_k_g, w_branch_na, w_branch_pool, w_branch_xa, w_out,
              ffn_norm_g, peer_w_q, peer_sub_keys, peer_u, peer_v):
    B, S, D = x.shape
    c0 = 3 * NA_WIDTH
    c1 = c0 + POOL_WIDTH
    c2 = c1 + XA_WIDTH
    h = x
    for l in range(DEPTH):
        xn = rms_norm(h, mix_norm_g[l])
        proj = xn @ w_in[l]
        qkv = proj[..., :c0].reshape(B, S, 3, NA_HEADS, NA_HEAD_DIM)
        q_na = rms_norm(qkv[:, :, 0], na_q_g[l])
        k_na = rms_norm(qkv[:, :, 1], na_k_g[l])
        y_na = neighbourhood_attention(q_na, k_na, qkv[:, :, 2], na_rpb[l])
        y_pool = multiscale_pool(proj[..., c0:c1], pool_w[l], pool_scale[l])
        mem_n = rms_norm(mem, mem_norm_g[l])
        y_xa = memory_attention(proj[..., c1:c2], mem_n, w_mem_kv[l], xa_q_g[l], xa_k_g[l])
        gate = jax.nn.sigmoid((proj[..., c2:] + gate_b[l]).astype(jnp.float32))
        gate = gate.reshape(B, S, N_BRANCH, D).astype(h.dtype)
        merged = (gate[:, :, 0] * (y_na @ w_branch_na[l])
                  + gate[:, :, 1] * (y_pool @ w_branch_pool[l])
                  + gate[:, :, 2] * (y_xa @ w_branch_xa[l]))
        h = h + merged @ w_out[l]
        hn = rms_norm(h, ffn_norm_g[l])
        h = h + peer_ffn(hn, peer_w_q[l], peer_sub_keys[l], peer_u[l], peer_v[l])
    return h
```

```python
import functools

import numpy as np
import jax
import jax.numpy as jnp
from jax import lax
from jax.experimental import pallas as pl
from jax.experimental.pallas import tpu as pltpu

GRID_W = 64
EPS = 1e-6
NEG_INF = -1e30

NA_HEADS = 8
NA_WIN_ROWS = 8
NA_WIN_COLS = 16
POOL_SIZES = (2, 4, 8, 16)
POOL_HALO = 8
XA_HEADS = 4
PEER_HEADS = 8
PEER_KEYS = 128
PEER_TOPK = 16

LANES = 128
TOKEN_TILE = 256
EXPERT_BLOCK = 1024
VMEM_LIMIT = 56 * 1024 * 1024

F32 = jnp.float32
BF16 = jnp.bfloat16
_NT = (((1,), (1,)), ((), ()))


def _rms_rows(x, g):
    return x * lax.rsqrt(jnp.mean(x * x, axis=-1, keepdims=True) + EPS) * g


def _group_mean_sq(v, ones_ref, width):
    sq = v * v
    hi = sq.astype(BF16)
    r1 = sq - hi.astype(F32)
    mid = r1.astype(BF16)
    lo = (r1 - mid.astype(F32)).astype(BF16)
    ones = ones_ref[...]
    tot = (jnp.dot(hi, ones, preferred_element_type=F32)
           + jnp.dot(mid, ones, preferred_element_type=F32)
           + jnp.dot(lo, ones, preferred_element_type=F32))
    return tot * (1.0 / width)


def _block_diag_ones(channels, width):
    idx = np.arange(channels) // width
    return jnp.asarray(idx[:, None] == idx[None, :], dtype=BF16)


def _in_proj_kernel(x_ref, g_ref, w_ref, qg_ref, kg_ref, xg_ref, ones_na_ref, ones_xa_ref,
                    q_out, k_out, v_out, p_out, xq_out, *, na_width, pool_width, xa_width,
                    na_head_dim, xa_head_dim):
    xb = _rms_rows(x_ref[...], g_ref[...]).astype(BF16)

    def proj(c0, c1):
        return jnp.dot(xb, w_ref[:, c0:c1], preferred_element_type=F32)

    c = 0
    q = proj(c, c + na_width); c += na_width
    k = proj(c, c + na_width); c += na_width
    v = proj(c, c + na_width); c += na_width
    p = proj(c, c + pool_width); c += pool_width
    xq = proj(c, c + xa_width)
    qn = q * lax.rsqrt(_group_mean_sq(q, ones_na_ref, na_head_dim) + EPS) * qg_ref[...]
    q_out[...] = (qn * (na_head_dim ** -0.5)).astype(BF16)
    kn = k * lax.rsqrt(_group_mean_sq(k, ones_na_ref, na_head_dim) + EPS) * kg_ref[...]
    k_out[...] = kn.astype(BF16)
    v_out[...] = v.astype(BF16)
    p_out[...] = p
    xqn = xq * lax.rsqrt(_group_mean_sq(xq, ones_xa_ref, xa_head_dim) + EPS) * xg_ref[...]
    xq_out[...] = xqn.astype(BF16)


def _in_proj(x2, g, w_cols, qg, kg, xg, na_width, pool_width, xa_width):
    n, d = x2.shape
    tm = TOKEN_TILE
    na_hd = na_width // NA_HEADS
    xa_hd = xa_width // XA_HEADS
    assert na_hd in (4, 16, 64), "the folded attention scale must be a power of two"
    cols = w_cols.shape[1]
    const = lambda i: (0, 0)
    tile = lambda i: (i, 0)
    kern = functools.partial(_in_proj_kernel, na_width=na_width, pool_width=pool_width,
                             xa_width=xa_width, na_head_dim=na_hd, xa_head_dim=xa_hd)
    return pl.pallas_call(
        kern,
        grid=(n // tm,),
        in_specs=[
            pl.BlockSpec((tm, d), tile),
            pl.BlockSpec((1, d), const),
            pl.BlockSpec((d, cols), const),
            pl.BlockSpec((1, na_width), const),
            pl.BlockSpec((1, na_width), const),
            pl.BlockSpec((1, xa_width), const),
            pl.BlockSpec((na_width, na_width), const),
            pl.BlockSpec((xa_width, xa_width), const),
        ],
        out_specs=[
            pl.BlockSpec((tm, na_width), tile),
            pl.BlockSpec((tm, na_width), tile),
            pl.BlockSpec((tm, na_width), tile),
            pl.BlockSpec((tm, pool_width), tile),
            pl.BlockSpec((tm, xa_width), tile),
        ],
        out_shape=[
            jax.ShapeDtypeStruct((n, na_width), BF16),
            jax.ShapeDtypeStruct((n, na_width), BF16),
            jax.ShapeDtypeStruct((n, na_width), BF16),
            jax.ShapeDtypeStruct((n, pool_width), F32),
            jax.ShapeDtypeStruct((n, xa_width), BF16),
        ],
        compiler_params=pltpu.CompilerParams(dimension_semantics=("parallel",),
                                             vmem_limit_bytes=VMEM_LIMIT),
        name="in_proj",
    )(x2, g, w_cols, qg, kg, xg, _block_diag_ones(na_width, na_hd), _block_diag_ones(xa_width, xa_hd))


def _mem_kv_kernel(mem_ref, g_ref, w_ref, kg_ref, ones_ref, k_out, v_out, *, xa_width, xa_head_dim):
    mn = _rms_rows(mem_ref[0], g_ref[...]).astype(BF16)
    kv = jnp.dot(mn, w_ref[...], preferred_element_type=F32)
    k = kv[:, :xa_width]
    kn = k * lax.rsqrt(_group_mean_sq(k, ones_ref, xa_head_dim) + EPS) * kg_ref[...]
    k_out[0] = kn.astype(BF16)
    v_out[0] = kv[:, xa_width:].astype(BF16)


def _mem_kv(mem, g, w_kv, kg, xa_width):
    b, m, d = mem.shape
    xa_hd = xa_width // XA_HEADS
    const = lambda i: (0, 0)
    kern = functools.partial(_mem_kv_kernel, xa_width=xa_width, xa_head_dim=xa_hd)
    return pl.pallas_call(
        kern,
        grid=(b,),
        in_specs=[
            pl.BlockSpec((1, m, d), lambda i: (i, 0, 0)),
            pl.BlockSpec((1, d), const),
            pl.BlockSpec((d, 2 * xa_width), const),
            pl.BlockSpec((1, xa_width), const),
            pl.BlockSpec((xa_width, xa_width), const),
        ],
        out_specs=[pl.BlockSpec((1, m, xa_width), lambda i: (i, 0, 0))] * 2,
        out_shape=[jax.ShapeDtypeStruct((b, m, xa_width), BF16)] * 2,
        compiler_params=pltpu.CompilerParams(dimension_semantics=("parallel",),
                                             vmem_limit_bytes=VMEM_LIMIT),
        name="mem_kv",
    )(mem, g, w_kv, kg, _block_diag_ones(xa_width, xa_hd))


def _na_bias_table(rpb):
    wr, w = NA_WIN_ROWS, GRID_W
    qc = np.arange(w)[:, None]
    kc = np.arange(w)[None, :]
    cs = np.clip(qc - NA_WIN_COLS // 2, 0, w - NA_WIN_COLS)
    col_ok = (kc >= cs) & (kc < cs + NA_WIN_COLS)
    dc = np.clip(kc - qc, -(NA_WIN_COLS - 1), NA_WIN_COLS - 1) + NA_WIN_COLS - 1
    d = np.arange(wr)[:, None]
    i = np.arange(wr)[None, :]
    dr = i - d + (NA_WIN_ROWS - 1)
    tab = rpb[:, dr[:, :, None, None], dc[None, None, :, :]]
    tab = jnp.where(col_ok[None, None, None], tab.astype(F32), NEG_INF)
    tab = tab.transpose(1, 0, 3, 2, 4)
    return tab.reshape(wr, rpb.shape[0], w, wr * w)


def _na_kernel(q_ref, k_ref, v_ref, bias_ref, o_ref, *, rows):
    r = pl.program_id(1)
    rs = jnp.clip(r - NA_WIN_ROWS // 2, 0, rows - NA_WIN_ROWS)
    start = pl.multiple_of(rs * GRID_W, GRID_W)
    win = NA_WIN_ROWS * GRID_W
    q = q_ref[0]
    lane = lax.broadcasted_iota(jnp.int32, (GRID_W, LANES), 1)
    head_dim = q.shape[-1] // NA_HEADS
    heads_per_tile = LANES // head_dim
    outs = []
    for t in range(NA_HEADS // heads_per_tile):
        cols = slice(t * LANES, (t + 1) * LANES)
        qt = q[:, cols]
        kt = k_ref[0, pl.ds(start, win), cols]
        vt = v_ref[0, pl.ds(start, win), cols]
        o_tile = None
        for sub in range(heads_per_tile):
            h = t * heads_per_tile + sub
            mine = (lane >= sub * head_dim) & (lane < (sub + 1) * head_dim)
            s = lax.dot_general(jnp.where(mine, qt, jnp.zeros_like(qt)), kt, _NT,
                                preferred_element_type=F32)
            s = s + bias_ref[0, h]
            e = jnp.exp(s - jnp.max(s, axis=-1, keepdims=True))
            p = e / jnp.sum(e, axis=-1, keepdims=True)
            o = jnp.dot(p.astype(BF16), vt, preferred_element_type=F32)
            o_tile = o if o_tile is None else jnp.where(mine, o, o_tile)
        outs.append(o_tile)
    o_ref[0] = jnp.concatenate(outs, axis=-1).astype(BF16)


def _na_attention(q, k, v, bias_tab, batch, seq):
    width = q.shape[-1]
    rows = seq // GRID_W
    assert rows >= NA_WIN_ROWS and LANES % (width // NA_HEADS) == 0
    q3, k3, v3 = (a.reshape(batch, seq, width) for a in (q, k, v))

    def bias_idx(b, r):
        rs = jnp.clip(r - NA_WIN_ROWS // 2, 0, rows - NA_WIN_ROWS)
        return (r - rs, 0, 0, 0)

    full = lambda b, r: (b, 0, 0)
    out = pl.pallas_call(
        functools.partial(_na_kernel, rows=rows),
        grid=(batch, rows),
        in_specs=[
            pl.BlockSpec((1, GRID_W, width), lambda b, r: (b, r, 0)),
            pl.BlockSpec((1, seq, width), full),
            pl.BlockSpec((1, seq, width), full),
            pl.BlockSpec((1, NA_HEADS, GRID_W, NA_WIN_ROWS * GRID_W), bias_idx),
        ],
        out_specs=pl.BlockSpec((1, GRID_W, width), lambda b, r: (b, r, 0)),
        out_shape=jax.ShapeDtypeStruct((batch, seq, width), BF16),
        compiler_params=pltpu.CompilerParams(dimension_semantics=("parallel", "arbitrary"),
                                             vmem_limit_bytes=VMEM_LIMIT),
        name="na_attn",
    )(q3, k3, v3, bias_tab)
    return out.reshape(batch * seq, width)


def _mix_kernel(x_ref, yna_ref, p_ref, pprev_ref, pnext_ref, xq_ref, km_ref, vm_ref,
                g_ref, wg_ref, gb_ref, pw_ref, ps_ref, wna_ref, wpool_ref, wxa_ref, wout_ref, fg_ref,
                h_out, hn_out, pbuf, *, seq, tiles_per_seq):
    tm, d = x_ref.shape
    ti = pl.program_id(0) % tiles_per_seq
    x = x_ref[...]
    xb = _rms_rows(x, g_ref[...]).astype(BF16)

    halo = POOL_HALO
    p = p_ref[...]
    pbuf[0:halo, :] = jnp.where(ti == 0, 0.0, pprev_ref[...])
    pbuf[halo:halo + tm, :] = p
    pbuf[halo + tm:2 * halo + tm, :] = jnp.where(ti == tiles_per_seq - 1, 0.0, pnext_ref[...])
    pos = ti * tm + lax.broadcasted_iota(jnp.int32, (tm, 1), 0)
    group = p.shape[-1] // len(POOL_SIZES)
    mixed = []
    for gi, w in enumerate(POOL_SIZES):
        cols = slice(gi * group, (gi + 1) * group)
        tot = pbuf[halo - w // 2:halo - w // 2 + tm, cols]
        for j in range(1 - w // 2, w // 2):
            tot = tot + pbuf[halo + j:halo + j + tm, cols]
        cnt = (jnp.minimum(pos + w // 2, seq) - jnp.maximum(pos - w // 2, 0)).astype(F32)
        pooled = tot / cnt - p[:, cols]
        mixed.append(jnp.dot(pooled.astype(BF16), pw_ref[gi], preferred_element_type=F32))
    y_pool = jnp.concatenate(mixed, axis=-1) * ps_ref[...]

    xq = xq_ref[...]
    xa_hd = xq.shape[-1] // XA_HEADS
    ys = []
    for h in range(XA_HEADS):
        cols = slice(h * xa_hd, (h + 1) * xa_hd)
        s = lax.dot_general(xq[:, cols], km_ref[0, :, cols], _NT, preferred_element_type=F32)
        s = s * (xa_hd ** -0.5)
        e = jnp.exp(s - jnp.max(s, axis=-1, keepdims=True))
        pr = e / jnp.sum(e, axis=-1, keepdims=True)
        ys.append(jnp.dot(pr.astype(BF16), vm_ref[0, :, cols], preferred_element_type=F32))
    y_xa = jnp.concatenate(ys, axis=-1)

    def gate(i):
        z = jnp.dot(xb, wg_ref[:, i * d:(i + 1) * d], preferred_element_type=F32)
        return jax.nn.sigmoid(z + gb_ref[:, i * d:(i + 1) * d])

    merged = (gate(0) * jnp.dot(yna_ref[...], wna_ref[...], preferred_element_type=F32)
              + gate(1) * jnp.dot(y_pool.astype(BF16), wpool_ref[...], preferred_element_type=F32)
              + gate(2) * jnp.dot(y_xa.astype(BF16), wxa_ref[...], preferred_element_type=F32))
    h = x + jnp.dot(merged.astype(BF16), wout_ref[...], preferred_element_type=F32)
    h_out[...] = h
    hn_out[...] = _rms_rows(h, fg_ref[...]).astype(BF16)


def _mix(x2, yna, p, xq, kmem, vmem, g, w_gate, gate_b, pool_w, pool_scale,
         w_na, w_pool, w_xa, w_out, ffn_g, seq):
    n, d = x2.shape
    tm = TOKEN_TILE
    assert seq % tm == 0 and tm % POOL_HALO == 0 and max(POOL_SIZES) // 2 == POOL_HALO
    tiles_per_seq = seq // tm
    halo_blocks = n // POOL_HALO
    per_tile = tm // POOL_HALO
    m = kmem.shape[1]
    na_w, pool_w_, xa_w = yna.shape[1], p.shape[1], xq.shape[1]
    const2 = lambda i: (0, 0)
    const3 = lambda i: (0, 0, 0)
    tile = lambda i: (i, 0)
    mem_idx = lambda i: (i // tiles_per_seq, 0, 0)
    kern = functools.partial(_mix_kernel, seq=seq, tiles_per_seq=tiles_per_seq)
    return pl.pallas_call(
        kern,
        grid=(n // tm,),
        in_specs=[
            pl.BlockSpec((tm, d), tile),
            pl.BlockSpec((tm, na_w), tile),
            pl.BlockSpec((tm, pool_w_), tile),
            pl.BlockSpec((POOL_HALO, pool_w_), lambda i: (jnp.maximum(i * per_tile - 1, 0), 0)),
            pl.BlockSpec((POOL_HALO, pool_w_), lambda i: (jnp.minimum((i + 1) * per_tile, halo_blocks - 1), 0)),
            pl.BlockSpec((tm, xa_w), tile),
            pl.BlockSpec((1, m, xa_w), mem_idx),
            pl.BlockSpec((1, m, xa_w), mem_idx),
            pl.BlockSpec((1, d), const2),
            pl.BlockSpec(w_gate.shape, const2),
            pl.BlockSpec(gate_b.shape, const2),
            pl.BlockSpec(pool_w.shape, const3),
            pl.BlockSpec((1, pool_w_), const2),
            pl.BlockSpec(w_na.shape, const2),
            pl.BlockSpec(w_pool.shape, const2),
            pl.BlockSpec(w_xa.shape, const2),
            pl.BlockSpec(w_out.shape, const2),
            pl.BlockSpec((1, d), const2),
        ],
        out_specs=[pl.BlockSpec((tm, d), tile), pl.BlockSpec((tm, d), tile)],
        out_shape=[jax.ShapeDtypeStruct((n, d), F32), jax.ShapeDtypeStruct((n, d), BF16)],
        scratch_shapes=[pltpu.VMEM((tm + 2 * POOL_HALO, pool_w_), F32)],
        compiler_params=pltpu.CompilerParams(dimension_semantics=("parallel",),
                                             vmem_limit_bytes=VMEM_LIMIT),
        name="mix",
    )(x2, yna, p, p, p, xq, kmem, vmem, g, w_gate, gate_b, pool_w, pool_scale,
      w_na, w_pool, w_xa, w_out, ffn_g)


def _gelu_exact(x):
    return 0.5 * x * (1.0 + lax.erf(x * (2.0 ** -0.5)))


def _candidate_pairs(depth):
    return [(i, j) for i in range(depth) for j in range(depth) if (i + 1) * (j + 1) <= depth]


def _peer_kernel(hn_ref, h_ref, wq_ref, sk_ref, u_ref, vt_ref, o_ref,
                 s2_ref, e2_ref, th_ref, e1_ref, top_ref, at_ref, wa_ref, acc_ref):
    j = pl.program_id(1)
    tt = hn_ref.shape[0]
    keys = PEER_KEYS
    depth = PEER_TOPK + 1
    groups = u_ref.shape[0] // keys

    @pl.when(j == 0)
    def _thresholds():
        q = jnp.dot(hn_ref[...], wq_ref[...], preferred_element_type=F32).astype(BF16)
        for h in range(PEER_HEADS):
            for half in range(2):
                hp = 2 * h + half
                s = lax.dot_general(sk_ref[hp], q[:, hp * keys:(hp + 1) * keys], _NT,
                                    preferred_element_type=F32)
                if half == 0:
                    e1_ref[h] = s
                else:
                    s2_ref[h] = s
                for r in range(depth):
                    m = jnp.max(s, axis=0, keepdims=True)
                    top_ref[half, r, h:h + 1, :] = m
                    s = jnp.where(s >= m, -jnp.inf, s)
        a = [top_ref[0, r] for r in range(depth)]
        b = [top_ref[1, r] for r in range(depth)]
        cands = [a[i] + b[k] for i, k in _candidate_pairs(depth)]
        best = []
        for r in range(depth):
            m = functools.reduce(jnp.maximum, cands)
            best.append(m)
            if r + 1 < depth:
                cands = [jnp.where(c >= m, -jnp.inf, c) for c in cands]
        kth, nxt = best[PEER_TOPK - 1], best[PEER_TOPK]
        tau = jnp.where(nxt > -jnp.inf, 0.5 * (kth + nxt), kth)
        z = functools.reduce(jnp.add, [jnp.exp(c - best[0]) for c in best[:PEER_TOPK]])
        top_ref[0, 1] = tau
        top_ref[0, 2] = 1.0 / z
        for h in range(PEER_HEADS):
            s1 = e1_ref[h]
            th_ref[h] = top_ref[0, 1, h:h + 1, :] - s1
            e1_ref[h] = jnp.exp(s1 - top_ref[0, 0, h:h + 1, :])
            e2_ref[h] = jnp.exp(s2_ref[h] - top_ref[1, 0, h:h + 1, :]) * top_ref[0, 2, h:h + 1, :]
        acc_ref[...] = jnp.zeros_like(acc_ref)

    at_ref[...] = lax.dot_general(u_ref[...], hn_ref[...], _NT, preferred_element_type=F32)

    rb = 16

    def group_body(l, carry):
        i1 = j * groups + l
        th = [th_ref[h, pl.ds(i1, 1), :] for h in range(PEER_HEADS)]
        e1 = [e1_ref[h, pl.ds(i1, 1), :] for h in range(PEER_HEADS)]
        for blk in range(keys // rb):
            rows = pl.ds(blk * rb, rb)
            dst = pl.ds(pl.multiple_of(l * keys + blk * rb, rb), rb)
            wt = jnp.zeros((rb, tt), F32)
            for h in range(PEER_HEADS):
                wt = wt + jnp.where(s2_ref[h, rows, :] >= th[h], e2_ref[h, rows, :], 0.0) * e1[h]
            wa_ref[dst, :] = (wt * _gelu_exact(at_ref[dst, :])).astype(BF16)
        return carry

    lax.fori_loop(0, groups, group_body, 0)
    acc_ref[...] += jnp.dot(vt_ref[...], wa_ref[...], preferred_element_type=F32)

    @pl.when(j == pl.num_programs(1) - 1)
    def _finish():
        o_ref[...] = h_ref[...] + acc_ref[...].T


def _peer(hn, h, w_q, sub_keys, u, v_t):
    n, d = hn.shape
    tt = TOKEN_TILE
    eb = EXPERT_BLOCK
    experts = u.shape[0]
    assert experts == PEER_KEYS * PEER_KEYS and experts % eb == 0 and eb % PEER_KEYS == 0
    qcols = w_q.shape[1]
    assert qcols == PEER_HEADS * 2 * PEER_KEYS, "the query sub-dimension must equal the lane width"
    depth = PEER_TOPK + 1
    return pl.pallas_call(
        _peer_kernel,
        grid=(n // tt, experts // eb),
        in_specs=[
            pl.BlockSpec((tt, d), lambda i, j: (i, 0)),
            pl.BlockSpec((tt, d), lambda i, j: (i, 0)),
            pl.BlockSpec((d, qcols), lambda i, j: (0, 0)),
            pl.BlockSpec(sub_keys.shape, lambda i, j: (0, 0, 0)),
            pl.BlockSpec((eb, d), lambda i, j: (j, 0)),
            pl.BlockSpec((d, eb), lambda i, j: (0, j)),
        ],
        out_specs=pl.BlockSpec((tt, d), lambda i, j: (i, 0)),
        out_shape=jax.ShapeDtypeStruct((n, d), F32),
        scratch_shapes=[
            pltpu.VMEM((PEER_HEADS, PEER_KEYS, tt), F32),
            pltpu.VMEM((PEER_HEADS, PEER_KEYS, tt), F32),
            pltpu.VMEM((PEER_HEADS, PEER_KEYS, tt), F32),
            pltpu.VMEM((PEER_HEADS, PEER_KEYS, tt), F32),
            pltpu.VMEM((2, depth, PEER_HEADS, tt), F32),
            pltpu.VMEM((eb, tt), F32),
            pltpu.VMEM((eb, tt), BF16),
            pltpu.VMEM((d, tt), F32),
        ],
        compiler_params=pltpu.CompilerParams(dimension_semantics=("parallel", "arbitrary"),
                                             vmem_limit_bytes=VMEM_LIMIT),
        name="peer",
    )(hn, h, w_q, sub_keys, u, v_t)


def kernel(x, mem, mix_norm_g, mem_norm_g, w_in, gate_b, w_mem_kv, na_q_g, na_k_g, na_rpb, pool_w, pool_scale, xa_q_g, xa_k_g, w_branch_na, w_branch_pool, w_branch_xa, w_out, ffn_norm_g, peer_w_q, peer_sub_keys, peer_u, peer_v):
    batch, seq, d = x.shape
    na_width = w_branch_na.shape[1]
    pool_width = w_branch_pool.shape[1]
    xa_width = w_branch_xa.shape[1]
    mix_cols = 3 * na_width + pool_width + xa_width
    n = batch * seq
    assert n % TOKEN_TILE == 0 and seq % GRID_W == 0
    row = lambda a: a.reshape(1, -1)
    h = x.reshape(n, d)
    for l in range(mix_norm_g.shape[0]):
        w_in_b = w_in[l].astype(BF16)
        q, k, v, p, xq = _in_proj(
            h, row(mix_norm_g[l]), w_in_b[:, :mix_cols],
            row(jnp.tile(na_q_g[l], NA_HEADS)), row(jnp.tile(na_k_g[l], NA_HEADS)),
            row(jnp.tile(xa_q_g[l], XA_HEADS)), na_width, pool_width, xa_width)
        kmem, vmem = _mem_kv(mem, row(mem_norm_g[l]), w_mem_kv[l].astype(BF16),
                             row(jnp.tile(xa_k_g[l], XA_HEADS)), xa_width)
        y_na = _na_attention(q, k, v, _na_bias_table(na_rpb[l]), batch, seq)
        h, hn = _mix(h, y_na, p, xq, kmem, vmem, row(mix_norm_g[l]), w_in_b[:, mix_cols:],
                     row(gate_b[l]), pool_w[l].astype(BF16), row(pool_scale[l]),
                     w_branch_na[l].astype(BF16), w_branch_pool[l].astype(BF16),
                     w_branch_xa[l].astype(BF16), w_out[l].astype(BF16), row(ffn_norm_g[l]), seq)
        sub_keys = peer_sub_keys[l].reshape(-1, *peer_sub_keys.shape[-2:]).astype(BF16)
        h = _peer(hn, h, peer_w_q[l].astype(BF16), sub_keys,
                  peer_u[l].astype(BF16), peer_v[l].T.astype(BF16))
    return h.reshape(batch, seq, d)
```

```python
import functools

import numpy as np
import jax
import jax.numpy as jnp
from jax import lax
from jax.experimental import pallas as pl
from jax.experimental.pallas import tpu as pltpu

GRID_W = 64
EPS = 1e-6
NEG_INF = -1e30

NA_HEADS = 8
NA_WIN_ROWS = 8
NA_WIN_COLS = 16
POOL_SIZES = (2, 4, 8, 16)
POOL_HALO = 8
XA_HEADS = 4
PEER_HEADS = 8
PEER_KEYS = 128
PEER_TOPK = 16

LANES = 128
TOKEN_TILE = 256
PEER_TOKEN_TILE = 512
EXPERT_BLOCK = 1024
VMEM_LIMIT = 56 * 1024 * 1024

F32 = jnp.float32
BF16 = jnp.bfloat16
_NT = (((1,), (1,)), ((), ()))


def _rms_rows(x, g):
    return x * lax.rsqrt(jnp.mean(x * x, axis=-1, keepdims=True) + EPS) * g


def _group_mean_sq(v, ones_ref, width):
    sq = v * v
    hi = sq.astype(BF16)
    r1 = sq - hi.astype(F32)
    mid = r1.astype(BF16)
    lo = (r1 - mid.astype(F32)).astype(BF16)
    ones = ones_ref[...]
    tot = (jnp.dot(hi, ones, preferred_element_type=F32)
           + jnp.dot(mid, ones, preferred_element_type=F32)
           + jnp.dot(lo, ones, preferred_element_type=F32))
    return tot * (1.0 / width)


def _block_diag_ones(channels, width):
    idx = np.arange(channels) // width
    return jnp.asarray(idx[:, None] == idx[None, :], dtype=BF16)


def _in_proj_kernel(x_ref, g_ref, w_ref, qg_ref, kg_ref, xg_ref, ones_na_ref, ones_xa_ref,
                    q_out, k_out, v_out, p_out, xq_out, *, na_width, pool_width, xa_width,
                    na_head_dim, xa_head_dim):
    xb = _rms_rows(x_ref[...], g_ref[...]).astype(BF16)

    def proj(c0, c1):
        return jnp.dot(xb, w_ref[:, c0:c1], preferred_element_type=F32)

    c = 0
    q = proj(c, c + na_width); c += na_width
    k = proj(c, c + na_width); c += na_width
    v = proj(c, c + na_width); c += na_width
    p = proj(c, c + pool_width); c += pool_width
    xq = proj(c, c + xa_width)
    qn = q * lax.rsqrt(_group_mean_sq(q, ones_na_ref, na_head_dim) + EPS) * qg_ref[...]
    q_out[...] = (qn * (na_head_dim ** -0.5)).astype(BF16)
    kn = k * lax.rsqrt(_group_mean_sq(k, ones_na_ref, na_head_dim) + EPS) * kg_ref[...]
    k_out[...] = kn.astype(BF16)
    v_out[...] = v.astype(BF16)
    p_out[...] = p
    xqn = xq * lax.rsqrt(_group_mean_sq(xq, ones_xa_ref, xa_head_dim) + EPS) * xg_ref[...]
    xq_out[...] = xqn.astype(BF16)


def _in_proj(x2, g, w_cols, qg, kg, xg, na_width, pool_width, xa_width):
    n, d = x2.shape
    tm = TOKEN_TILE
    na_hd = na_width // NA_HEADS
    xa_hd = xa_width // XA_HEADS
    assert na_hd in (4, 16, 64), "the folded attention scale must be a power of two"
    cols = w_cols.shape[1]
    const = lambda i: (0, 0)
    tile = lambda i: (i, 0)
    kern = functools.partial(_in_proj_kernel, na_width=na_width, pool_width=pool_width,
                             xa_width=xa_width, na_head_dim=na_hd, xa_head_dim=xa_hd)
    return pl.pallas_call(
        kern,
        grid=(n // tm,),
        in_specs=[
            pl.BlockSpec((tm, d), tile),
            pl.BlockSpec((1, d), const),
            pl.BlockSpec((d, cols), const),
            pl.BlockSpec((1, na_width), const),
            pl.BlockSpec((1, na_width), const),
            pl.BlockSpec((1, xa_width), const),
            pl.BlockSpec((na_width, na_width), const),
            pl.BlockSpec((xa_width, xa_width), const),
        ],
        out_specs=[
            pl.BlockSpec((tm, na_width), tile),
            pl.BlockSpec((tm, na_width), tile),
            pl.BlockSpec((tm, na_width), tile),
            pl.BlockSpec((tm, pool_width), tile),
            pl.BlockSpec((tm, xa_width), tile),
        ],
        out_shape=[
            jax.ShapeDtypeStruct((n, na_width), BF16),
            jax.ShapeDtypeStruct((n, na_width), BF16),
            jax.ShapeDtypeStruct((n, na_width), BF16),
            jax.ShapeDtypeStruct((n, pool_width), F32),
            jax.ShapeDtypeStruct((n, xa_width), BF16),
        ],
        compiler_params=pltpu.CompilerParams(dimension_semantics=("parallel",),
                                             vmem_limit_bytes=VMEM_LIMIT),
        name="in_proj",
    )(x2, g, w_cols, qg, kg, xg, _block_diag_ones(na_width, na_hd), _block_diag_ones(xa_width, xa_hd))


def _mem_kv_kernel(mem_ref, g_ref, w_ref, kg_ref, ones_ref, k_out, v_out, *, xa_width, xa_head_dim):
    mn = _rms_rows(mem_ref[0], g_ref[...]).astype(BF16)
    kv = jnp.dot(mn, w_ref[...], preferred_element_type=F32)
    k = kv[:, :xa_width]
    kn = k * lax.rsqrt(_group_mean_sq(k, ones_ref, xa_head_dim) + EPS) * kg_ref[...]
    k_out[0] = kn.astype(BF16)
    v_out[0] = kv[:, xa_width:].astype(BF16)


def _mem_kv(mem, g, w_kv, kg, xa_width):
    b, m, d = mem.shape
    xa_hd = xa_width // XA_HEADS
    const = lambda i: (0, 0)
    kern = functools.partial(_mem_kv_kernel, xa_width=xa_width, xa_head_dim=xa_hd)
    return pl.pallas_call(
        kern,
        grid=(b,),
        in_specs=[
            pl.BlockSpec((1, m, d), lambda i: (i, 0, 0)),
            pl.BlockSpec((1, d), const),
            pl.BlockSpec((d, 2 * xa_width), const),
            pl.BlockSpec((1, xa_width), const),
            pl.BlockSpec((xa_width, xa_width), const),
        ],
        out_specs=[pl.BlockSpec((1, m, xa_width), lambda i: (i, 0, 0))] * 2,
        out_shape=[jax.ShapeDtypeStruct((b, m, xa_width), BF16)] * 2,
        compiler_params=pltpu.CompilerParams(dimension_semantics=("parallel",),
                                             vmem_limit_bytes=VMEM_LIMIT),
        name="mem_kv",
    )(mem, g, w_kv, kg, _block_diag_ones(xa_width, xa_hd))


def _na_bias_table(rpb):
    wr, w = NA_WIN_ROWS, GRID_W
    qc = np.arange(w)[:, None]
    kc = np.arange(w)[None, :]
    cs = np.clip(qc - NA_WIN_COLS // 2, 0, w - NA_WIN_COLS)
    col_ok = (kc >= cs) & (kc < cs + NA_WIN_COLS)
    pad = w - NA_WIN_COLS
    ext = jnp.pad(rpb.astype(F32), ((0, 0), (0, 0), (pad, pad)), mode="edge")
    by_q = jnp.stack([ext[:, :, w - 1 - c:2 * w - 1 - c] for c in range(w)], axis=2)
    by_q = jnp.where(col_ok[None, None], by_q, NEG_INF)
    tab = jnp.stack([by_q[:, wr - 1 - d:2 * wr - 1 - d] for d in range(wr)], axis=0)
    tab = tab.transpose(0, 1, 3, 2, 4)
    return tab.reshape(wr, rpb.shape[0], w, wr * w)


def _na_kernel(q_ref, k_ref, v_ref, bias_ref, o_ref, *, rows):
    r = pl.program_id(1)
    rs = jnp.clip(r - NA_WIN_ROWS // 2, 0, rows - NA_WIN_ROWS)
    start = pl.multiple_of(rs * GRID_W, GRID_W)
    win = NA_WIN_ROWS * GRID_W
    q = q_ref[0]
    lane = lax.broadcasted_iota(jnp.int32, (GRID_W, LANES), 1)
    head_dim = q.shape[-1] // NA_HEADS
    heads_per_tile = LANES // head_dim
    outs = []
    for t in range(NA_HEADS // heads_per_tile):
        cols = slice(t * LANES, (t + 1) * LANES)
        qt = q[:, cols]
        kt = k_ref[0, pl.ds(start, win), cols]
        vt = v_ref[0, pl.ds(start, win), cols]
        o_tile = None
        for sub in range(heads_per_tile):
            h = t * heads_per_tile + sub
            mine = (lane >= sub * head_dim) & (lane < (sub + 1) * head_dim)
            s = lax.dot_general(jnp.where(mine, qt, jnp.zeros_like(qt)), kt, _NT,
                                preferred_element_type=F32)
            s = s + bias_ref[0, h]
            e = jnp.exp(s - jnp.max(s, axis=-1, keepdims=True))
            p = e / jnp.sum(e, axis=-1, keepdims=True)
            o = jnp.dot(p.astype(BF16), vt, preferred_element_type=F32)
            o_tile = o if o_tile is None else jnp.where(mine, o, o_tile)
        outs.append(o_tile)
    o_ref[0] = jnp.concatenate(outs, axis=-1).astype(BF16)


def _na_attention(q, k, v, bias_tab, batch, seq):
    width = q.shape[-1]
    rows = seq // GRID_W
    assert rows >= NA_WIN_ROWS and LANES % (width // NA_HEADS) == 0
    q3, k3, v3 = (a.reshape(batch, seq, width) for a in (q, k, v))

    def bias_idx(b, r):
        rs = jnp.clip(r - NA_WIN_ROWS // 2, 0, rows - NA_WIN_ROWS)
        return (r - rs, 0, 0, 0)

    full = lambda b, r: (b, 0, 0)
    out = pl.pallas_call(
        functools.partial(_na_kernel, rows=rows),
        grid=(batch, rows),
        in_specs=[
            pl.BlockSpec((1, GRID_W, width), lambda b, r: (b, r, 0)),
            pl.BlockSpec((1, seq, width), full),
            pl.BlockSpec((1, seq, width), full),
            pl.BlockSpec((1, NA_HEADS, GRID_W, NA_WIN_ROWS * GRID_W), bias_idx),
        ],
        out_specs=pl.BlockSpec((1, GRID_W, width), lambda b, r: (b, r, 0)),
        out_shape=jax.ShapeDtypeStruct((batch, seq, width), BF16),
        compiler_params=pltpu.CompilerParams(dimension_semantics=("parallel", "arbitrary"),
                                             vmem_limit_bytes=VMEM_LIMIT),
        name="na_attn",
    )(q3, k3, v3, bias_tab)
    return out.reshape(batch * seq, width)


def _mix_kernel(x_ref, yna_ref, p_ref, pprev_ref, pnext_ref, xq_ref, km_ref, vm_ref,
                g_ref, wg_ref, gb_ref, pw_ref, ps_ref, wna_ref, wpool_ref, wxa_ref, wout_ref, fg_ref,
                h_out, hn_out, pbuf, *, seq, tiles_per_seq):
    tm, d = x_ref.shape
    ti = pl.program_id(0) % tiles_per_seq
    x = x_ref[...]
    xb = _rms_rows(x, g_ref[...]).astype(BF16)

    halo = POOL_HALO
    p = p_ref[...]
    pbuf[0:halo, :] = jnp.where(ti == 0, 0.0, pprev_ref[...])
    pbuf[halo:halo + tm, :] = p
    pbuf[halo + tm:2 * halo + tm, :] = jnp.where(ti == tiles_per_seq - 1, 0.0, pnext_ref[...])
    pos = ti * tm + lax.broadcasted_iota(jnp.int32, (tm, 1), 0)
    group = p.shape[-1] // len(POOL_SIZES)
    mixed = []
    for gi, w in enumerate(POOL_SIZES):
        cols = slice(gi * group, (gi + 1) * group)
        tot = pbuf[halo - w // 2:halo - w // 2 + tm, cols]
        for j in range(1 - w // 2, w // 2):
            tot = tot + pbuf[halo + j:halo + j + tm, cols]
        cnt = (jnp.minimum(pos + w // 2, seq) - jnp.maximum(pos - w // 2, 0)).astype(F32)
        pooled = tot / cnt - p[:, cols]
        mixed.append(jnp.dot(pooled.astype(BF16), pw_ref[gi], preferred_element_type=F32))
    y_pool = jnp.concatenate(mixed, axis=-1) * ps_ref[...]

    xq = xq_ref[...]
    xa_hd = xq.shape[-1] // XA_HEADS
    ys = []
    for h in range(XA_HEADS):
        cols = slice(h * xa_hd, (h + 1) * xa_hd)
        s = lax.dot_general(xq[:, cols], km_ref[0, :, cols], _NT, preferred_element_type=F32)
        s = s * (xa_hd ** -0.5)
        e = jnp.exp(s - jnp.max(s, axis=-1, keepdims=True))
        pr = e / jnp.sum(e, axis=-1, keepdims=True)
        ys.append(jnp.dot(pr.astype(BF16), vm_ref[0, :, cols], preferred_element_type=F32))
    y_xa = jnp.concatenate(ys, axis=-1)

    def gate(i):
        z = jnp.dot(xb, wg_ref[:, i * d:(i + 1) * d], preferred_element_type=F32)
        return jax.nn.sigmoid(z + gb_ref[:, i * d:(i + 1) * d])

    merged = (gate(0) * jnp.dot(yna_ref[...], wna_ref[...], preferred_element_type=F32)
              + gate(1) * jnp.dot(y_pool.astype(BF16), wpool_ref[...], preferred_element_type=F32)
              + gate(2) * jnp.dot(y_xa.astype(BF16), wxa_ref[...], preferred_element_type=F32))
    h = x + jnp.dot(merged.astype(BF16), wout_ref[...], preferred_element_type=F32)
    h_out[...] = h
    hn_out[...] = _rms_rows(h, fg_ref[...]).astype(BF16)


def _mix(x2, yna, p, xq, kmem, vmem, g, w_gate, gate_b, pool_w, pool_scale,
         w_na, w_pool, w_xa, w_out, ffn_g, seq):
    n, d = x2.shape
    tm = TOKEN_TILE
    assert seq % tm == 0 and tm % POOL_HALO == 0 and max(POOL_SIZES) // 2 == POOL_HALO
    tiles_per_seq = seq // tm
    halo_blocks = n // POOL_HALO
    per_tile = tm // POOL_HALO
    m = kmem.shape[1]
    na_w, pool_w_, xa_w = yna.shape[1], p.shape[1], xq.shape[1]
    const2 = lambda i: (0, 0)
    const3 = lambda i: (0, 0, 0)
    tile = lambda i: (i, 0)
    mem_idx = lambda i: (i // tiles_per_seq, 0, 0)
    kern = functools.partial(_mix_kernel, seq=seq, tiles_per_seq=tiles_per_seq)
    return pl.pallas_call(
        kern,
        grid=(n // tm,),
        in_specs=[
            pl.BlockSpec((tm, d), tile),
            pl.BlockSpec((tm, na_w), tile),
            pl.BlockSpec((tm, pool_w_), tile),
            pl.BlockSpec((POOL_HALO, pool_w_), lambda i: (jnp.maximum(i * per_tile - 1, 0), 0)),
            pl.BlockSpec((POOL_HALO, pool_w_), lambda i: (jnp.minimum((i + 1) * per_tile, halo_blocks - 1), 0)),
            pl.BlockSpec((tm, xa_w), tile),
            pl.BlockSpec((1, m, xa_w), mem_idx),
            pl.BlockSpec((1, m, xa_w), mem_idx),
            pl.BlockSpec((1, d), const2),
            pl.BlockSpec(w_gate.shape, const2),
            pl.BlockSpec(gate_b.shape, const2),
            pl.BlockSpec(pool_w.shape, const3),
            pl.BlockSpec((1, pool_w_), const2),
            pl.BlockSpec(w_na.shape, const2),
            pl.BlockSpec(w_pool.shape, const2),
            pl.BlockSpec(w_xa.shape, const2),
            pl.BlockSpec(w_out.shape, const2),
            pl.BlockSpec((1, d), const2),
        ],
        out_specs=[pl.BlockSpec((tm, d), tile), pl.BlockSpec((tm, d), tile)],
        out_shape=[jax.ShapeDtypeStruct((n, d), F32), jax.ShapeDtypeStruct((n, d), BF16)],
        scratch_shapes=[pltpu.VMEM((tm + 2 * POOL_HALO, pool_w_), F32)],
        compiler_params=pltpu.CompilerParams(dimension_semantics=("parallel",),
                                             vmem_limit_bytes=VMEM_LIMIT),
        name="mix",
    )(x2, yna, p, p, p, xq, kmem, vmem, g, w_gate, gate_b, pool_w, pool_scale,
      w_na, w_pool, w_xa, w_out, ffn_g)


def _gelu_exact(x):
    return 0.5 * x * (1.0 + lax.erf(x * (2.0 ** -0.5)))


def _candidate_pairs(depth):
    return [(i, j) for i in range(depth) for j in range(depth) if (i + 1) * (j + 1) <= depth]


def _peer_kernel(hn_ref, h_ref, wq_ref, sk_ref, u_ref, vt_ref, o_ref,
                 s2_ref, e2_ref, th_ref, e1_ref, top_ref, at_ref, wa_ref, acc_ref):
    j = pl.program_id(1)
    tt = hn_ref.shape[0]
    keys = PEER_KEYS
    depth = PEER_TOPK + 1
    groups = u_ref.shape[0] // keys

    @pl.when(j == 0)
    def _thresholds():
        q = jnp.dot(hn_ref[...], wq_ref[...], preferred_element_type=F32).astype(BF16)
        for h in range(PEER_HEADS):
            for half in range(2):
                hp = 2 * h + half
                s = lax.dot_general(sk_ref[hp], q[:, hp * keys:(hp + 1) * keys], _NT,
                                    preferred_element_type=F32)
                if half == 0:
                    e1_ref[h] = s
                else:
                    s2_ref[h] = s
                for r in range(depth):
                    m = jnp.max(s, axis=0, keepdims=True)
                    top_ref[half, r, h:h + 1, :] = m
                    s = jnp.where(s >= m, -jnp.inf, s)
        a = [top_ref[0, r] for r in range(depth)]
        b = [top_ref[1, r] for r in range(depth)]
        cands = [a[i] + b[k] for i, k in _candidate_pairs(depth)]
        best = []
        for r in range(depth):
            m = functools.reduce(jnp.maximum, cands)
            best.append(m)
            if r + 1 < depth:
                cands = [jnp.where(c >= m, -jnp.inf, c) for c in cands]
        kth, nxt = best[PEER_TOPK - 1], best[PEER_TOPK]
        tau = jnp.where(nxt > -jnp.inf, 0.5 * (kth + nxt), kth)
        z = functools.reduce(jnp.add, [jnp.exp(c - best[0]) for c in best[:PEER_TOPK]])
        top_ref[0, 1] = tau
        top_ref[0, 2] = 1.0 / z
        for h in range(PEER_HEADS):
            s1 = e1_ref[h]
            th_ref[h] = top_ref[0, 1, h:h + 1, :] - s1
            e1_ref[h] = jnp.exp(s1 - top_ref[0, 0, h:h + 1, :])
            e2_ref[h] = jnp.exp(s2_ref[h] - top_ref[1, 0, h:h + 1, :]) * top_ref[0, 2, h:h + 1, :]
        acc_ref[...] = jnp.zeros_like(acc_ref)

    at_ref[...] = lax.dot_general(u_ref[...], hn_ref[...], _NT, preferred_element_type=F32)

    rb = 16

    def group_body(l, carry):
        i1 = j * groups + l
        th = [th_ref[h, pl.ds(i1, 1), :] for h in range(PEER_HEADS)]
        e1 = [e1_ref[h, pl.ds(i1, 1), :] for h in range(PEER_HEADS)]
        for blk in range(keys // rb):
            rows = pl.ds(blk * rb, rb)
            dst = pl.ds(pl.multiple_of(l * keys + blk * rb, rb), rb)
            wt = jnp.zeros((rb, tt), F32)
            for h in range(PEER_HEADS):
                wt = wt + jnp.where(s2_ref[h, rows, :] >= th[h], e2_ref[h, rows, :], 0.0) * e1[h]
            wa_ref[dst, :] = (wt * _gelu_exact(at_ref[dst, :])).astype(BF16)
        return carry

    lax.fori_loop(0, groups, group_body, 0)
    acc_ref[...] += jnp.dot(vt_ref[...], wa_ref[...], preferred_element_type=F32)

    @pl.when(j == pl.num_programs(1) - 1)
    def _finish():
        o_ref[...] = h_ref[...] + acc_ref[...].T


def _peer(hn, h, w_q, sub_keys, u, v_t):
    n, d = hn.shape
    tt = PEER_TOKEN_TILE
    eb = EXPERT_BLOCK
    experts = u.shape[0]
    assert experts == PEER_KEYS * PEER_KEYS and experts % eb == 0 and eb % PEER_KEYS == 0
    qcols = w_q.shape[1]
    assert qcols == PEER_HEADS * 2 * PEER_KEYS, "the query sub-dimension must equal the lane width"
    depth = PEER_TOPK + 1
    return pl.pallas_call(
        _peer_kernel,
        grid=(n // tt, experts // eb),
        in_specs=[
            pl.BlockSpec((tt, d), lambda i, j: (i, 0)),
            pl.BlockSpec((tt, d), lambda i, j: (i, 0)),
            pl.BlockSpec((d, qcols), lambda i, j: (0, 0)),
            pl.BlockSpec(sub_keys.shape, lambda i, j: (0, 0, 0)),
            pl.BlockSpec((eb, d), lambda i, j: (j, 0)),
            pl.BlockSpec((d, eb), lambda i, j: (0, j)),
        ],
        out_specs=pl.BlockSpec((tt, d), lambda i, j: (i, 0)),
        out_shape=jax.ShapeDtypeStruct((n, d), F32),
        scratch_shapes=[
            pltpu.VMEM((PEER_HEADS, PEER_KEYS, tt), F32),
            pltpu.VMEM((PEER_HEADS, PEER_KEYS, tt), F32),
            pltpu.VMEM((PEER_HEADS, PEER_KEYS, tt), F32),
            pltpu.VMEM((PEER_HEADS, PEER_KEYS, tt), F32),
            pltpu.VMEM((2, depth, PEER_HEADS, tt), F32),
            pltpu.VMEM((eb, tt), F32),
            pltpu.VMEM((eb, tt), BF16),
            pltpu.VMEM((d, tt), F32),
        ],
        compiler_params=pltpu.CompilerParams(dimension_semantics=("parallel", "arbitrary"),
                                             vmem_limit_bytes=VMEM_LIMIT),
        name="peer",
    )(hn, h, w_q, sub_keys, u, v_t)


def kernel(x, mem, mix_norm_g, mem_norm_g, w_in, gate_b, w_mem_kv, na_q_g, na_k_g, na_rpb, pool_w, pool_scale, xa_q_g, xa_k_g, w_branch_na, w_branch_pool, w_branch_xa, w_out, ffn_norm_g, peer_w_q, peer_sub_keys, peer_u, peer_v):
    batch, seq, d = x.shape
    na_width = w_branch_na.shape[1]
    pool_width = w_branch_pool.shape[1]
    xa_width = w_branch_xa.shape[1]
    mix_cols = 3 * na_width + pool_width + xa_width
    n = batch * seq
    assert n % TOKEN_TILE == 0 and n % PEER_TOKEN_TILE == 0 and seq % GRID_W == 0
    row = lambda a: a.reshape(1, -1)
    h = x.reshape(n, d)
    for l in range(mix_norm_g.shape[0]):
        w_in_b = w_in[l].astype(BF16)
        q, k, v, p, xq = _in_proj(
            h, row(mix_norm_g[l]), w_in_b[:, :mix_cols],
            row(jnp.tile(na_q_g[l], NA_HEADS)), row(jnp.tile(na_k_g[l], NA_HEADS)),
            row(jnp.tile(xa_q_g[l], XA_HEADS)), na_width, pool_width, xa_width)
        kmem, vmem = _mem_kv(mem, row(mem_norm_g[l]), w_mem_kv[l].astype(BF16),
                             row(jnp.tile(xa_k_g[l], XA_HEADS)), xa_width)
        y_na = _na_attention(q, k, v, _na_bias_table(na_rpb[l]), batch, seq)
        h, hn = _mix(h, y_na, p, xq, kmem, vmem, row(mix_norm_g[l]), w_in_b[:, mix_cols:],
                     row(gate_b[l]), pool_w[l].astype(BF16), row(pool_scale[l]),
                     w_branch_na[l].astype(BF16), w_branch_pool[l].astype(BF16),
                     w_branch_xa[l].astype(BF16), w_out[l].astype(BF16), row(ffn_norm_g[l]), seq)
        sub_keys = peer_sub_keys[l].reshape(-1, *peer_sub_keys.shape[-2:]).astype(BF16)
        h = _peer(hn, h, peer_w_q[l].astype(BF16), sub_keys,
                  peer_u[l].astype(BF16), peer_v[l].T.astype(BF16))
    return h.reshape(batch, seq, d)
```

```python
import functools

import numpy as np
import jax
import jax.numpy as jnp
from jax import lax
from jax.experimental import pallas as pl
from jax.experimental.pallas import tpu as pltpu

GRID_W = 64
EPS = 1e-6
NEG_INF = -1e30

NA_HEADS = 8
NA_WIN_ROWS = 8
NA_WIN_COLS = 16
NA_ROWS_PER_STEP = 4
POOL_SIZES = (2, 4, 8, 16)
POOL_HALO = 8
XA_HEADS = 4
PEER_HEADS = 8
PEER_KEYS = 128
PEER_TOPK = 16

LANES = 128
TOKEN_TILE = 256
PEER_TOKEN_TILE = 512
EXPERT_BLOCK = 1024
VMEM_LIMIT = 56 * 1024 * 1024

F32 = jnp.float32
BF16 = jnp.bfloat16
_NT = (((1,), (1,)), ((), ()))


def _rms_rows(x, g):
    return x * lax.rsqrt(jnp.mean(x * x, axis=-1, keepdims=True) + EPS) * g


def _group_mean_sq(v, ones_ref, width):
    sq = v * v
    hi = sq.astype(BF16)
    r1 = sq - hi.astype(F32)
    mid = r1.astype(BF16)
    lo = (r1 - mid.astype(F32)).astype(BF16)
    ones = ones_ref[...]
    tot = (jnp.dot(hi, ones, preferred_element_type=F32)
           + jnp.dot(mid, ones, preferred_element_type=F32)
           + jnp.dot(lo, ones, preferred_element_type=F32))
    return tot * (1.0 / width)


def _block_diag_ones(channels, width):
    idx = np.arange(channels) // width
    return jnp.asarray(idx[:, None] == idx[None, :], dtype=BF16)


def _in_proj_kernel(x_ref, g_ref, w_ref, qg_ref, kg_ref, xg_ref, ones_na_ref, ones_xa_ref,
                    q_out, k_out, v_out, p_out, xq_out, *, na_width, pool_width, xa_width,
                    na_head_dim, xa_head_dim):
    xb = _rms_rows(x_ref[...], g_ref[...]).astype(BF16)

    def proj(c0, c1):
        return jnp.dot(xb, w_ref[:, c0:c1], preferred_element_type=F32)

    c = 0
    q = proj(c, c + na_width); c += na_width
    k = proj(c, c + na_width); c += na_width
    v = proj(c, c + na_width); c += na_width
    p = proj(c, c + pool_width); c += pool_width
    xq = proj(c, c + xa_width)
    qn = q * lax.rsqrt(_group_mean_sq(q, ones_na_ref, na_head_dim) + EPS) * qg_ref[...]
    q_out[...] = (qn * (na_head_dim ** -0.5)).astype(BF16)
    kn = k * lax.rsqrt(_group_mean_sq(k, ones_na_ref, na_head_dim) + EPS) * kg_ref[...]
    k_out[...] = kn.astype(BF16)
    v_out[...] = v.astype(BF16)
    p_out[...] = p
    xqn = xq * lax.rsqrt(_group_mean_sq(xq, ones_xa_ref, xa_head_dim) + EPS) * xg_ref[...]
    xq_out[...] = xqn.astype(BF16)


def _in_proj(x2, g, w_cols, qg, kg, xg, na_width, pool_width, xa_width):
    n, d = x2.shape
    tm = TOKEN_TILE
    na_hd = na_width // NA_HEADS
    xa_hd = xa_width // XA_HEADS
    assert na_hd in (4, 16, 64), "the folded attention scale must be a power of two"
    cols = w_cols.shape[1]
    const = lambda i: (0, 0)
    tile = lambda i: (i, 0)
    kern = functools.partial(_in_proj_kernel, na_width=na_width, pool_width=pool_width,
                             xa_width=xa_width, na_head_dim=na_hd, xa_head_dim=xa_hd)
    return pl.pallas_call(
        kern,
        grid=(n // tm,),
        in_specs=[
            pl.BlockSpec((tm, d), tile),
            pl.BlockSpec((1, d), const),
            pl.BlockSpec((d, cols), const),
            pl.BlockSpec((1, na_width), const),
            pl.BlockSpec((1, na_width), const),
            pl.BlockSpec((1, xa_width), const),
            pl.BlockSpec((na_width, na_width), const),
            pl.BlockSpec((xa_width, xa_width), const),
        ],
        out_specs=[
            pl.BlockSpec((tm, na_width), tile),
            pl.BlockSpec((tm, na_width), tile),
            pl.BlockSpec((tm, na_width), tile),
            pl.BlockSpec((tm, pool_width), tile),
            pl.BlockSpec((tm, xa_width), tile),
        ],
        out_shape=[
            jax.ShapeDtypeStruct((n, na_width), BF16),
            jax.ShapeDtypeStruct((n, na_width), BF16),
            jax.ShapeDtypeStruct((n, na_width), BF16),
            jax.ShapeDtypeStruct((n, pool_width), F32),
            jax.ShapeDtypeStruct((n, xa_width), BF16),
        ],
        compiler_params=pltpu.CompilerParams(dimension_semantics=("parallel",),
                                             vmem_limit_bytes=VMEM_LIMIT),
        name="in_proj",
    )(x2, g, w_cols, qg, kg, xg, _block_diag_ones(na_width, na_hd), _block_diag_ones(xa_width, xa_hd))


def _mem_kv_kernel(mem_ref, g_ref, w_ref, kg_ref, ones_ref, k_out, v_out, *, xa_width, xa_head_dim):
    mn = _rms_rows(mem_ref[0], g_ref[...]).astype(BF16)
    kv = jnp.dot(mn, w_ref[...], preferred_element_type=F32)
    k = kv[:, :xa_width]
    kn = k * lax.rsqrt(_group_mean_sq(k, ones_ref, xa_head_dim) + EPS) * kg_ref[...]
    k_out[0] = kn.astype(BF16)
    v_out[0] = kv[:, xa_width:].astype(BF16)


def _mem_kv(mem, g, w_kv, kg, xa_width):
    b, m, d = mem.shape
    xa_hd = xa_width // XA_HEADS
    const = lambda i: (0, 0)
    kern = functools.partial(_mem_kv_kernel, xa_width=xa_width, xa_head_dim=xa_hd)
    return pl.pallas_call(
        kern,
        grid=(b,),
        in_specs=[
            pl.BlockSpec((1, m, d), lambda i: (i, 0, 0)),
            pl.BlockSpec((1, d), const),
            pl.BlockSpec((d, 2 * xa_width), const),
            pl.BlockSpec((1, xa_width), const),
            pl.BlockSpec((xa_width, xa_width), const),
        ],
        out_specs=[pl.BlockSpec((1, m, xa_width), lambda i: (i, 0, 0))] * 2,
        out_shape=[jax.ShapeDtypeStruct((b, m, xa_width), BF16)] * 2,
        compiler_params=pltpu.CompilerParams(dimension_semantics=("parallel",),
                                             vmem_limit_bytes=VMEM_LIMIT),
        name="mem_kv",
    )(mem, g, w_kv, kg, _block_diag_ones(xa_width, xa_hd))


def _na_bias_table(rpb):
    wr, w = NA_WIN_ROWS, GRID_W
    qc = np.arange(w)[:, None]
    kc = np.arange(w)[None, :]
    cs = np.clip(qc - NA_WIN_COLS // 2, 0, w - NA_WIN_COLS)
    col_ok = (kc >= cs) & (kc < cs + NA_WIN_COLS)
    pad = w - NA_WIN_COLS
    ext = jnp.pad(rpb.astype(F32), ((0, 0), (0, 0), (pad, pad)), mode="edge")
    by_q = jnp.stack([ext[:, :, w - 1 - c:2 * w - 1 - c] for c in range(w)], axis=2)
    by_q = jnp.where(col_ok[None, None], by_q, NEG_INF)
    tab = jnp.stack([by_q[:, wr - 1 - d:2 * wr - 1 - d] for d in range(wr)], axis=0)
    tab = tab.transpose(0, 1, 3, 2, 4)
    return tab.reshape(wr, rpb.shape[0], w, wr * w)


def _na_union_bias(bias_tab, rows):
    rb, wr, w = NA_ROWS_PER_STEP, NA_WIN_ROWS, GRID_W
    union = wr + rb
    heads = bias_tab.shape[1]
    old = bias_tab.reshape(wr, heads, w, wr, w)
    neg = lambda n: jnp.full((heads, w, n, w), NEG_INF, F32)
    first = [(rq, 0) for rq in range(rb)]
    interior = [(rb, rq) for rq in range(rb)]
    last = [(rb + rq, rb) for rq in range(rb)]
    types = []
    for pattern in (first, interior, last):
        per_row = [jnp.concatenate([neg(off), old[d], neg(union - wr - off)], axis=2)
                   for d, off in pattern]
        t = jnp.stack(per_row, axis=1)
        t = t.reshape(heads // 2, 2 * rb * w, union * w)
        types.append(t)
    return jnp.stack(types)


def _na_kernel(q_ref, k_ref, v_ref, bias_ref, o_ref, *, rows):
    blk = pl.program_id(1)
    rb = NA_ROWS_PER_STEP
    union = (NA_WIN_ROWS + rb) * GRID_W
    ks = jnp.clip(blk * rb - NA_WIN_ROWS // 2, 0, rows - NA_WIN_ROWS - rb)
    start = pl.multiple_of(ks * GRID_W, GRID_W)
    head_dim = q_ref.shape[-1] // NA_HEADS
    tokens = rb * GRID_W
    low = lax.broadcasted_iota(jnp.int32, (tokens, LANES), 1) < head_dim
    outs = []
    for t in range(q_ref.shape[-1] // LANES):
        cols = slice(t * LANES, (t + 1) * LANES)
        qt = q_ref[0, :, cols]
        kt = k_ref[0, pl.ds(start, union), cols]
        vt = v_ref[0, pl.ds(start, union), cols]
        zero = jnp.zeros_like(qt)
        q2 = jnp.concatenate([jnp.where(low, qt, zero), jnp.where(low, zero, qt)], axis=0)
        s = lax.dot_general(q2, kt, _NT, preferred_element_type=F32) + bias_ref[0, t]
        e = jnp.exp(s - jnp.max(s, axis=-1, keepdims=True))
        p = e / jnp.sum(e, axis=-1, keepdims=True)
        o2 = jnp.dot(p.astype(BF16), vt, preferred_element_type=F32)
        outs.append(jnp.where(low, o2[:tokens], o2[tokens:]))
    o_ref[0] = jnp.concatenate(outs, axis=-1).astype(BF16)


def _na_attention(q, k, v, bias_tab, batch, seq):
    width = q.shape[-1]
    rows = seq // GRID_W
    rb = NA_ROWS_PER_STEP
    assert LANES == 2 * (width // NA_HEADS), "two heads per lane tile"
    assert rb == NA_WIN_ROWS // 2 and rows % rb == 0 and rows >= NA_WIN_ROWS + 2 * rb
    nblk = rows // rb
    q3, k3, v3 = (a.reshape(batch, seq, width) for a in (q, k, v))
    tab = _na_union_bias(bias_tab, rows)

    def bias_idx(b, i):
        return (jnp.where(i == 0, 0, jnp.where(i == nblk - 1, 2, 1)), 0, 0, 0)

    full = lambda b, i: (b, 0, 0)
    once = pl.Buffered(1)
    out = pl.pallas_call(
        functools.partial(_na_kernel, rows=rows),
        grid=(batch, nblk),
        in_specs=[
            pl.BlockSpec((1, rb * GRID_W, width), lambda b, i: (b, i, 0)),
            pl.BlockSpec((1, seq, width), full, pipeline_mode=once),
            pl.BlockSpec((1, seq, width), full, pipeline_mode=once),
            pl.BlockSpec((1,) + tab.shape[1:], bias_idx, pipeline_mode=once),
        ],
        out_specs=pl.BlockSpec((1, rb * GRID_W, width), lambda b, i: (b, i, 0)),
        out_shape=jax.ShapeDtypeStruct((batch, seq, width), BF16),
        compiler_params=pltpu.CompilerParams(dimension_semantics=("parallel", "arbitrary"),
                                             vmem_limit_bytes=VMEM_LIMIT),
        name="na_attn",
    )(q3, k3, v3, tab)
    return out.reshape(batch * seq, width)


def _mix_kernel(x_ref, yna_ref, p_ref, pprev_ref, pnext_ref, xq_ref, km_ref, vm_ref,
                g_ref, wg_ref, gb_ref, pw_ref, ps_ref, wna_ref, wpool_ref, wxa_ref, wout_ref, fg_ref,
                h_out, hn_out, pbuf, *, seq, tiles_per_seq):
    tm, d = x_ref.shape
    ti = pl.program_id(0) % tiles_per_seq
    x = x_ref[...]
    xb = _rms_rows(x, g_ref[...]).astype(BF16)

    halo = POOL_HALO
    p = p_ref[...]
    pbuf[0:halo, :] = jnp.where(ti == 0, 0.0, pprev_ref[...])
    pbuf[halo:halo + tm, :] = p
    pbuf[halo + tm:2 * halo + tm, :] = jnp.where(ti == tiles_per_seq - 1, 0.0, pnext_ref[...])
    pos = ti * tm + lax.broadcasted_iota(jnp.int32, (tm, 1), 0)
    group = p.shape[-1] // len(POOL_SIZES)
    mixed = []
    for gi, w in enumerate(POOL_SIZES):
        cols = slice(gi * group, (gi + 1) * group)
        tot = pbuf[halo - w // 2:halo - w // 2 + tm, cols]
        for j in range(1 - w // 2, w // 2):
            tot = tot + pbuf[halo + j:halo + j + tm, cols]
        cnt = (jnp.minimum(pos + w // 2, seq) - jnp.maximum(pos - w // 2, 0)).astype(F32)
        pooled = tot / cnt - p[:, cols]
        mixed.append(jnp.dot(pooled.astype(BF16), pw_ref[gi], preferred_element_type=F32))
    y_pool = jnp.concatenate(mixed, axis=-1) * ps_ref[...]

    xq = xq_ref[...]
    xa_hd = xq.shape[-1] // XA_HEADS
    ys = []
    for h in range(XA_HEADS):
        cols = slice(h * xa_hd, (h + 1) * xa_hd)
        s = lax.dot_general(xq[:, cols], km_ref[0, :, cols], _NT, preferred_element_type=F32)
        s = s * (xa_hd ** -0.5)
        e = jnp.exp(s - jnp.max(s, axis=-1, keepdims=True))
        pr = e / jnp.sum(e, axis=-1, keepdims=True)
        ys.append(jnp.dot(pr.astype(BF16), vm_ref[0, :, cols], preferred_element_type=F32))
    y_xa = jnp.concatenate(ys, axis=-1)

    def gate(i):
        z = jnp.dot(xb, wg_ref[:, i * d:(i + 1) * d], preferred_element_type=F32)
        return jax.nn.sigmoid(z + gb_ref[:, i * d:(i + 1) * d])

    merged = (gate(0) * jnp.dot(yna_ref[...], wna_ref[...], preferred_element_type=F32)
              + gate(1) * jnp.dot(y_pool.astype(BF16), wpool_ref[...], preferred_element_type=F32)
              + gate(2) * jnp.dot(y_xa.astype(BF16), wxa_ref[...], preferred_element_type=F32))
    h = x + jnp.dot(merged.astype(BF16), wout_ref[...], preferred_element_type=F32)
    h_out[...] = h
    hn_out[...] = _rms_rows(h, fg_ref[...]).astype(BF16)


def _mix(x2, yna, p, xq, kmem, vmem, g, w_gate, gate_b, pool_w, pool_scale,
         w_na, w_pool, w_xa, w_out, ffn_g, seq):
    n, d = x2.shape
    tm = TOKEN_TILE
    assert seq % tm == 0 and tm % POOL_HALO == 0 and max(POOL_SIZES) // 2 == POOL_HALO
    tiles_per_seq = seq // tm
    halo_blocks = n // POOL_HALO
    per_tile = tm // POOL_HALO
    m = kmem.shape[1]
    na_w, pool_w_, xa_w = yna.shape[1], p.shape[1], xq.shape[1]
    const2 = lambda i: (0, 0)
    const3 = lambda i: (0, 0, 0)
    tile = lambda i: (i, 0)
    mem_idx = lambda i: (i // tiles_per_seq, 0, 0)
    kern = functools.partial(_mix_kernel, seq=seq, tiles_per_seq=tiles_per_seq)
    return pl.pallas_call(
        kern,
        grid=(n // tm,),
        in_specs=[
            pl.BlockSpec((tm, d), tile),
            pl.BlockSpec((tm, na_w), tile),
            pl.BlockSpec((tm, pool_w_), tile),
            pl.BlockSpec((POOL_HALO, pool_w_), lambda i: (jnp.maximum(i * per_tile - 1, 0), 0)),
            pl.BlockSpec((POOL_HALO, pool_w_), lambda i: (jnp.minimum((i + 1) * per_tile, halo_blocks - 1), 0)),
            pl.BlockSpec((tm, xa_w), tile),
            pl.BlockSpec((1, m, xa_w), mem_idx),
            pl.BlockSpec((1, m, xa_w), mem_idx),
            pl.BlockSpec((1, d), const2),
            pl.BlockSpec(w_gate.shape, const2),
            pl.BlockSpec(gate_b.shape, const2),
            pl.BlockSpec(pool_w.shape, const3),
            pl.BlockSpec((1, pool_w_), const2),
            pl.BlockSpec(w_na.shape, const2),
            pl.BlockSpec(w_pool.shape, const2),
            pl.BlockSpec(w_xa.shape, const2),
            pl.BlockSpec(w_out.shape, const2),
            pl.BlockSpec((1, d), const2),
        ],
        out_specs=[pl.BlockSpec((tm, d), tile), pl.BlockSpec((tm, d), tile)],
        out_shape=[jax.ShapeDtypeStruct((n, d), F32), jax.ShapeDtypeStruct((n, d), BF16)],
        scratch_shapes=[pltpu.VMEM((tm + 2 * POOL_HALO, pool_w_), F32)],
        compiler_params=pltpu.CompilerParams(dimension_semantics=("parallel",),
                                             vmem_limit_bytes=VMEM_LIMIT),
        name="mix",
    )(x2, yna, p, p, p, xq, kmem, vmem, g, w_gate, gate_b, pool_w, pool_scale,
      w_na, w_pool, w_xa, w_out, ffn_g)


def _gelu_exact(x):
    return 0.5 * x * (1.0 + lax.erf(x * (2.0 ** -0.5)))


def _candidate_pairs(depth):
    return [(i, j) for i in range(depth) for j in range(depth) if (i + 1) * (j + 1) <= depth]


def _peer_kernel(hn_ref, h_ref, wq_ref, sk_ref, u0_ref, un_ref, vt_ref, o_ref,
                 s2_ref, e2_ref, th_ref, e1_ref, top_ref, at0_ref, at1_ref, wa_ref, acc_ref):
    j = pl.program_id(1)
    tt = hn_ref.shape[0]
    keys = PEER_KEYS
    depth = PEER_TOPK + 1
    eb = un_ref.shape[0]
    groups = eb // keys

    def pre_activations(u_blk):
        return lax.dot_general(u_blk, hn_ref[...], _NT, preferred_element_type=F32)

    @pl.when(j == 0)
    def _thresholds():
        q = jnp.dot(hn_ref[...], wq_ref[...], preferred_element_type=F32).astype(BF16)
        for h in range(PEER_HEADS):
            for half in range(2):
                hp = 2 * h + half
                s = lax.dot_general(sk_ref[hp], q[:, hp * keys:(hp + 1) * keys], _NT,
                                    preferred_element_type=F32)
                if half == 0:
                    e1_ref[h] = s
                else:
                    s2_ref[h] = s
                for r in range(depth):
                    m = jnp.max(s, axis=0, keepdims=True)
                    top_ref[half, r, h:h + 1, :] = m
                    s = jnp.where(s >= m, -jnp.inf, s)
        a = [top_ref[0, r] for r in range(depth)]
        b = [top_ref[1, r] for r in range(depth)]
        cands = [a[i] + b[k] for i, k in _candidate_pairs(depth)]
        best = []
        for r in range(depth):
            m = functools.reduce(jnp.maximum, cands)
            best.append(m)
            if r + 1 < depth:
                cands = [jnp.where(c >= m, -jnp.inf, c) for c in cands]
        kth, nxt = best[PEER_TOPK - 1], best[PEER_TOPK]
        tau = jnp.where(nxt > -jnp.inf, 0.5 * (kth + nxt), kth)
        z = functools.reduce(jnp.add, [jnp.exp(c - best[0]) for c in best[:PEER_TOPK]])
        top_ref[0, 1] = tau
        top_ref[0, 2] = 1.0 / z
        for h in range(PEER_HEADS):
            s1 = e1_ref[h]
            th_ref[h] = top_ref[0, 1, h:h + 1, :] - s1
            e1_ref[h] = jnp.exp(s1 - top_ref[0, 0, h:h + 1, :])
            e2_ref[h] = jnp.exp(s2_ref[h] - top_ref[1, 0, h:h + 1, :]) * top_ref[0, 2, h:h + 1, :]
        acc_ref[...] = jnp.zeros_like(acc_ref)
        at0_ref[...] = pre_activations(u0_ref[...])

    rb = 16
    half_rows = eb // 2

    def step(at_cur, at_next):
        at_next[...] = pre_activations(un_ref[...])
        for l in range(groups):
            i1 = j * groups + l
            th = [th_ref[h, pl.ds(i1, 1), :] for h in range(PEER_HEADS)]
            e1 = [e1_ref[h, pl.ds(i1, 1), :] for h in range(PEER_HEADS)]
            for blk in range(keys // rb):
                rows = pl.ds(blk * rb, rb)
                dst = pl.ds(l * keys + blk * rb, rb)
                wt = jnp.zeros((rb, tt), F32)
                for h in range(PEER_HEADS):
                    wt = wt + jnp.where(s2_ref[h, rows, :] >= th[h], e2_ref[h, rows, :], 0.0) * e1[h]
                wa_ref[dst, :] = (wt * _gelu_exact(at_cur[dst, :])).astype(BF16)
            if (l + 1) * keys == half_rows:
                acc_ref[...] += jnp.dot(vt_ref[:, :half_rows], wa_ref[:half_rows, :],
                                        preferred_element_type=F32)
        acc_ref[...] += jnp.dot(vt_ref[:, half_rows:], wa_ref[half_rows:, :],
                                preferred_element_type=F32)

    @pl.when(j % 2 == 0)
    def _even():
        step(at0_ref, at1_ref)

    @pl.when(j % 2 == 1)
    def _odd():
        step(at1_ref, at0_ref)

    @pl.when(j == pl.num_programs(1) - 1)
    def _finish():
        o_ref[...] = h_ref[...] + acc_ref[...].T


def _peer(hn, h, w_q, sub_keys, u, v_t):
    n, d = hn.shape
    tt = PEER_TOKEN_TILE
    eb = EXPERT_BLOCK
    experts = u.shape[0]
    assert experts == PEER_KEYS * PEER_KEYS and experts % eb == 0 and eb % PEER_KEYS == 0
    qcols = w_q.shape[1]
    assert qcols == PEER_HEADS * 2 * PEER_KEYS, "the query sub-dimension must equal the lane width"
    depth = PEER_TOPK + 1
    return pl.pallas_call(
        _peer_kernel,
        grid=(n // tt, experts // eb),
        in_specs=[
            pl.BlockSpec((tt, d), lambda i, j: (i, 0)),
            pl.BlockSpec((tt, d), lambda i, j: (i, 0)),
            pl.BlockSpec((d, qcols), lambda i, j: (0, 0)),
            pl.BlockSpec(sub_keys.shape, lambda i, j: (0, 0, 0)),
            pl.BlockSpec((eb, d), lambda i, j: (0, 0)),
            pl.BlockSpec((eb, d), lambda i, j: (jnp.minimum(j + 1, experts // eb - 1), 0)),
            pl.BlockSpec((d, eb), lambda i, j: (0, j)),
        ],
        out_specs=pl.BlockSpec((tt, d), lambda i, j: (i, 0)),
        out_shape=jax.ShapeDtypeStruct((n, d), F32),
        scratch_shapes=[
            pltpu.VMEM((PEER_HEADS, PEER_KEYS, tt), F32),
            pltpu.VMEM((PEER_HEADS, PEER_KEYS, tt), F32),
            pltpu.VMEM((PEER_HEADS, PEER_KEYS, tt), F32),
            pltpu.VMEM((PEER_HEADS, PEER_KEYS, tt), F32),
            pltpu.VMEM((2, depth, PEER_HEADS, tt), F32),
            pltpu.VMEM((eb, tt), F32),
            pltpu.VMEM((eb, tt), F32),
            pltpu.VMEM((eb, tt), BF16),
            pltpu.VMEM((d, tt), F32),
        ],
        compiler_params=pltpu.CompilerParams(dimension_semantics=("parallel", "arbitrary"),
                                             vmem_limit_bytes=VMEM_LIMIT),
        name="peer",
    )(hn, h, w_q, sub_keys, u, u, v_t)


def kernel(x, mem, mix_norm_g, mem_norm_g, w_in, gate_b, w_mem_kv, na_q_g, na_k_g, na_rpb, pool_w, pool_scale, xa_q_g, xa_k_g, w_branch_na, w_branch_pool, w_branch_xa, w_out, ffn_norm_g, peer_w_q, peer_sub_keys, peer_u, peer_v):
    batch, seq, d = x.shape
    na_width = w_branch_na.shape[1]
    pool_width = w_branch_pool.shape[1]
    xa_width = w_branch_xa.shape[1]
    mix_cols = 3 * na_width + pool_width + xa_width
    n = batch * seq
    assert n % TOKEN_TILE == 0 and n % PEER_TOKEN_TILE == 0 and seq % GRID_W == 0
    row = lambda a: a.reshape(1, -1)
    h = x.reshape(n, d)
    for l in range(mix_norm_g.shape[0]):
        w_in_b = w_in[l].astype(BF16)
        q, k, v, p, xq = _in_proj(
            h, row(mix_norm_g[l]), w_in_b[:, :mix_cols],
            row(jnp.tile(na_q_g[l], NA_HEADS)), row(jnp.tile(na_k_g[l], NA_HEADS)),
            row(jnp.tile(xa_q_g[l], XA_HEADS)), na_width, pool_width, xa_width)
        kmem, vmem = _mem_kv(mem, row(mem_norm_g[l]), w_mem_kv[l].astype(BF16),
                             row(jnp.tile(xa_k_g[l], XA_HEADS)), xa_width)
        y_na = _na_attention(q, k, v, _na_bias_table(na_rpb[l]), batch, seq)
        h, hn = _mix(h, y_na, p, xq, kmem, vmem, row(mix_norm_g[l]), w_in_b[:, mix_cols:],
                     row(gate_b[l]), pool_w[l].astype(BF16), row(pool_scale[l]),
                     w_branch_na[l].astype(BF16), w_branch_pool[l].astype(BF16),
                     w_branch_xa[l].astype(BF16), w_out[l].astype(BF16), row(ffn_norm_g[l]), seq)
        sub_keys = peer_sub_keys[l].reshape(-1, *peer_sub_keys.shape[-2:]).astype(BF16)
        h = _peer(hn, h, peer_w_q[l].astype(BF16), sub_keys,
                  peer_u[l].astype(BF16), peer_v[l].T.astype(BF16))
    return h.reshape(batch, seq, d)
```

```python
import functools

import numpy as np
import jax
import jax.numpy as jnp
from jax import lax
from jax.experimental import pallas as pl
from jax.experimental.pallas import tpu as pltpu

GRID_W = 64
EPS = 1e-6
NEG_INF = -1e30

NA_HEADS = 8
NA_WIN_ROWS = 8
NA_WIN_COLS = 16
NA_ROWS_PER_STEP = 4
POOL_SIZES = (2, 4, 8, 16)
POOL_HALO = 8
XA_HEADS = 4
PEER_HEADS = 8
PEER_KEYS = 128
PEER_TOPK = 16

LANES = 128
TOKEN_TILE = 256
PEER_TOKEN_TILE = 512
EXPERT_BLOCK = 1024
VMEM_LIMIT = 56 * 1024 * 1024

F32 = jnp.float32
BF16 = jnp.bfloat16
_NT = (((1,), (1,)), ((), ()))


def _rms_rows(x, g):
    return x * lax.rsqrt(jnp.mean(x * x, axis=-1, keepdims=True) + EPS) * g


def _group_mean_sq(v, ones_ref, width):
    sq = v * v
    hi = sq.astype(BF16)
    r1 = sq - hi.astype(F32)
    mid = r1.astype(BF16)
    lo = (r1 - mid.astype(F32)).astype(BF16)
    ones = ones_ref[...]
    tot = (jnp.dot(hi, ones, preferred_element_type=F32)
           + jnp.dot(mid, ones, preferred_element_type=F32)
           + jnp.dot(lo, ones, preferred_element_type=F32))
    return tot * (1.0 / width)


def _block_diag_ones(channels, width):
    idx = np.arange(channels) // width
    return jnp.asarray(idx[:, None] == idx[None, :], dtype=BF16)


def _in_proj_kernel(x_ref, g_ref, w_ref, qg_ref, kg_ref, xg_ref, ones_na_ref, ones_xa_ref,
                    q_out, k_out, v_out, p_out, xq_out, *, na_width, pool_width, xa_width,
                    na_head_dim, xa_head_dim):
    xb = _rms_rows(x_ref[...], g_ref[...]).astype(BF16)

    def proj(c0, c1):
        return jnp.dot(xb, w_ref[:, c0:c1], preferred_element_type=F32)

    c = 0
    q = proj(c, c + na_width); c += na_width
    k = proj(c, c + na_width); c += na_width
    v = proj(c, c + na_width); c += na_width
    p = proj(c, c + pool_width); c += pool_width
    xq = proj(c, c + xa_width)
    qn = q * lax.rsqrt(_group_mean_sq(q, ones_na_ref, na_head_dim) + EPS) * qg_ref[...]
    q_out[...] = (qn * (na_head_dim ** -0.5)).astype(BF16)
    kn = k * lax.rsqrt(_group_mean_sq(k, ones_na_ref, na_head_dim) + EPS) * kg_ref[...]
    k_out[...] = kn.astype(BF16)
    v_out[...] = v.astype(BF16)
    p_out[...] = p
    xqn = xq * lax.rsqrt(_group_mean_sq(xq, ones_xa_ref, xa_head_dim) + EPS) * xg_ref[...]
    xq_out[...] = xqn.astype(BF16)


def _in_proj(x2, g, w_cols, qg, kg, xg, na_width, pool_width, xa_width):
    n, d = x2.shape
    tm = TOKEN_TILE
    na_hd = na_width // NA_HEADS
    xa_hd = xa_width // XA_HEADS
    assert na_hd in (4, 16, 64), "the folded attention scale must be a power of two"
    cols = w_cols.shape[1]
    const = lambda i: (0, 0)
    tile = lambda i: (i, 0)
    kern = functools.partial(_in_proj_kernel, na_width=na_width, pool_width=pool_width,
                             xa_width=xa_width, na_head_dim=na_hd, xa_head_dim=xa_hd)
    return pl.pallas_call(
        kern,
        grid=(n // tm,),
        in_specs=[
            pl.BlockSpec((tm, d), tile),
            pl.BlockSpec((1, d), const),
            pl.BlockSpec((d, cols), const),
            pl.BlockSpec((1, na_width), const),
            pl.BlockSpec((1, na_width), const),
            pl.BlockSpec((1, xa_width), const),
            pl.BlockSpec((na_width, na_width), const),
            pl.BlockSpec((xa_width, xa_width), const),
        ],
        out_specs=[
            pl.BlockSpec((tm, na_width), tile),
            pl.BlockSpec((tm, na_width), tile),
            pl.BlockSpec((tm, na_width), tile),
            pl.BlockSpec((tm, pool_width), tile),
            pl.BlockSpec((tm, xa_width), tile),
        ],
        out_shape=[
            jax.ShapeDtypeStruct((n, na_width), BF16),
            jax.ShapeDtypeStruct((n, na_width), BF16),
            jax.ShapeDtypeStruct((n, na_width), BF16),
            jax.ShapeDtypeStruct((n, pool_width), F32),
            jax.ShapeDtypeStruct((n, xa_width), BF16),
        ],
        compiler_params=pltpu.CompilerParams(dimension_semantics=("parallel",),
                                             vmem_limit_bytes=VMEM_LIMIT),
        name="in_proj",
    )(x2, g, w_cols, qg, kg, xg, _block_diag_ones(na_width, na_hd), _block_diag_ones(xa_width, xa_hd))


def _mem_kv_kernel(mem_ref, g_ref, w_ref, kg_ref, ones_ref, k_out, v_out, *, xa_width, xa_head_dim):
    mn = _rms_rows(mem_ref[0], g_ref[...]).astype(BF16)
    kv = jnp.dot(mn, w_ref[...], preferred_element_type=F32)
    k = kv[:, :xa_width]
    kn = k * lax.rsqrt(_group_mean_sq(k, ones_ref, xa_head_dim) + EPS) * kg_ref[...]
    k_out[0] = kn.astype(BF16)
    v_out[0] = kv[:, xa_width:].astype(BF16)


def _mem_kv(mem, g, w_kv, kg, xa_width):
    b, m, d = mem.shape
    xa_hd = xa_width // XA_HEADS
    const = lambda i: (0, 0)
    kern = functools.partial(_mem_kv_kernel, xa_width=xa_width, xa_head_dim=xa_hd)
    return pl.pallas_call(
        kern,
        grid=(b,),
        in_specs=[
            pl.BlockSpec((1, m, d), lambda i: (i, 0, 0)),
            pl.BlockSpec((1, d), const),
            pl.BlockSpec((d, 2 * xa_width), const),
            pl.BlockSpec((1, xa_width), const),
            pl.BlockSpec((xa_width, xa_width), const),
        ],
        out_specs=[pl.BlockSpec((1, m, xa_width), lambda i: (i, 0, 0))] * 2,
        out_shape=[jax.ShapeDtypeStruct((b, m, xa_width), BF16)] * 2,
        compiler_params=pltpu.CompilerParams(dimension_semantics=("parallel",),
                                             vmem_limit_bytes=VMEM_LIMIT),
        name="mem_kv",
    )(mem, g, w_kv, kg, _block_diag_ones(xa_width, xa_hd))


def _na_bias_table(rpb):
    wr, w = NA_WIN_ROWS, GRID_W
    qc = np.arange(w)[:, None]
    kc = np.arange(w)[None, :]
    cs = np.clip(qc - NA_WIN_COLS // 2, 0, w - NA_WIN_COLS)
    col_ok = (kc >= cs) & (kc < cs + NA_WIN_COLS)
    pad = w - NA_WIN_COLS
    ext = jnp.pad(rpb.astype(F32), ((0, 0), (0, 0), (pad, pad)), mode="edge")
    by_q = jnp.stack([ext[:, :, w - 1 - c:2 * w - 1 - c] for c in range(w)], axis=2)
    by_q = jnp.where(col_ok[None, None], by_q, NEG_INF)
    tab = jnp.stack([by_q[:, wr - 1 - d:2 * wr - 1 - d] for d in range(wr)], axis=0)
    tab = tab.transpose(0, 1, 3, 2, 4)
    return tab.reshape(wr, rpb.shape[0], w, wr * w)


def _na_union_bias(bias_tab, rows):
    rb, wr, w = NA_ROWS_PER_STEP, NA_WIN_ROWS, GRID_W
    union = wr + rb
    heads = bias_tab.shape[1]
    old = bias_tab.reshape(wr, heads, w, wr, w)
    neg = lambda n: jnp.full((heads, w, n, w), NEG_INF, F32)
    first = [(rq, 0) for rq in range(rb)]
    interior = [(rb, rq) for rq in range(rb)]
    last = [(rb + rq, rb) for rq in range(rb)]
    types = []
    for pattern in (first, interior, last):
        per_row = [jnp.concatenate([neg(off), old[d], neg(union - wr - off)], axis=2)
                   for d, off in pattern]
        t = jnp.stack(per_row, axis=1)
        t = t.reshape(heads // 2, 2 * rb * w, union * w)
        types.append(t)
    return jnp.stack(types)


def _na_kernel(q_ref, k_ref, v_ref, bias_ref, o_ref, *, rows):
    blk = pl.program_id(1)
    rb = NA_ROWS_PER_STEP
    union = (NA_WIN_ROWS + rb) * GRID_W
    ks = jnp.clip(blk * rb - NA_WIN_ROWS // 2, 0, rows - NA_WIN_ROWS - rb)
    start = pl.multiple_of(ks * GRID_W, GRID_W)
    head_dim = q_ref.shape[-1] // NA_HEADS
    tokens = rb * GRID_W
    low = lax.broadcasted_iota(jnp.int32, (tokens, LANES), 1) < head_dim
    outs = []
    for t in range(q_ref.shape[-1] // LANES):
        cols = slice(t * LANES, (t + 1) * LANES)
        qt = q_ref[0, :, cols]
        kt = k_ref[0, pl.ds(start, union), cols]
        vt = v_ref[0, pl.ds(start, union), cols]
        zero = jnp.zeros_like(qt)
        q2 = jnp.concatenate([jnp.where(low, qt, zero), jnp.where(low, zero, qt)], axis=0)
        s = lax.dot_general(q2, kt, _NT, preferred_element_type=F32) + bias_ref[0, t]
        e = jnp.exp(s - jnp.max(s, axis=-1, keepdims=True))
        p = e / jnp.sum(e, axis=-1, keepdims=True)
        o2 = jnp.dot(p.astype(BF16), vt, preferred_element_type=F32)
        outs.append(jnp.where(low, o2[:tokens], o2[tokens:]))
    o_ref[0] = jnp.concatenate(outs, axis=-1).astype(BF16)


def _na_attention(q, k, v, bias_tab, batch, seq):
    width = q.shape[-1]
    rows = seq // GRID_W
    rb = NA_ROWS_PER_STEP
    assert LANES == 2 * (width // NA_HEADS), "two heads per lane tile"
    assert rb == NA_WIN_ROWS // 2 and rows % rb == 0 and rows >= NA_WIN_ROWS + 2 * rb
    nblk = rows // rb
    q3, k3, v3 = (a.reshape(batch, seq, width) for a in (q, k, v))
    tab = _na_union_bias(bias_tab, rows)

    def bias_idx(b, i):
        return (jnp.where(i == 0, 0, jnp.where(i == nblk - 1, 2, 1)), 0, 0, 0)

    full = lambda b, i: (b, 0, 0)
    once = pl.Buffered(1)
    out = pl.pallas_call(
        functools.partial(_na_kernel, rows=rows),
        grid=(batch, nblk),
        in_specs=[
            pl.BlockSpec((1, rb * GRID_W, width), lambda b, i: (b, i, 0)),
            pl.BlockSpec((1, seq, width), full, pipeline_mode=once),
            pl.BlockSpec((1, seq, width), full, pipeline_mode=once),
            pl.BlockSpec((1,) + tab.shape[1:], bias_idx, pipeline_mode=once),
        ],
        out_specs=pl.BlockSpec((1, rb * GRID_W, width), lambda b, i: (b, i, 0)),
        out_shape=jax.ShapeDtypeStruct((batch, seq, width), BF16),
        compiler_params=pltpu.CompilerParams(dimension_semantics=("parallel", "arbitrary"),
                                             vmem_limit_bytes=VMEM_LIMIT),
        name="na_attn",
    )(q3, k3, v3, tab)
    return out.reshape(batch * seq, width)


def _mix_kernel(x_ref, yna_ref, p_ref, pprev_ref, pnext_ref, xq_ref, km_ref, vm_ref,
                g_ref, wg_ref, gb_ref, pw_ref, ps_ref, wna_ref, wpool_ref, wxa_ref, wout_ref, fg_ref,
                h_out, hn_out, pbuf, *, seq, tiles_per_seq):
    tm, d = x_ref.shape
    ti = pl.program_id(0) % tiles_per_seq
    x = x_ref[...]
    xb = _rms_rows(x, g_ref[...]).astype(BF16)

    halo = POOL_HALO
    p = p_ref[...]
    pbuf[0:halo, :] = jnp.where(ti == 0, 0.0, pprev_ref[...])
    pbuf[halo:halo + tm, :] = p
    pbuf[halo + tm:2 * halo + tm, :] = jnp.where(ti == tiles_per_seq - 1, 0.0, pnext_ref[...])
    pos = ti * tm + lax.broadcasted_iota(jnp.int32, (tm, 1), 0)
    group = p.shape[-1] // len(POOL_SIZES)
    mixed = []
    for gi, w in enumerate(POOL_SIZES):
        cols = slice(gi * group, (gi + 1) * group)
        tot = pbuf[halo - w // 2:halo - w // 2 + tm, cols]
        for j in range(1 - w // 2, w // 2):
            tot = tot + pbuf[halo + j:halo + j + tm, cols]
        cnt = (jnp.minimum(pos + w // 2, seq) - jnp.maximum(pos - w // 2, 0)).astype(F32)
        pooled = tot / cnt - p[:, cols]
        mixed.append(jnp.dot(pooled.astype(BF16), pw_ref[gi], preferred_element_type=F32))
    y_pool = jnp.concatenate(mixed, axis=-1) * ps_ref[...]

    xq = xq_ref[...]
    xa_hd = xq.shape[-1] // XA_HEADS
    ys = []
    for h in range(XA_HEADS):
        cols = slice(h * xa_hd, (h + 1) * xa_hd)
        s = lax.dot_general(xq[:, cols], km_ref[0, :, cols], _NT, preferred_element_type=F32)
        s = s * (xa_hd ** -0.5)
        e = jnp.exp(s - jnp.max(s, axis=-1, keepdims=True))
        pr = e / jnp.sum(e, axis=-1, keepdims=True)
        ys.append(jnp.dot(pr.astype(BF16), vm_ref[0, :, cols], preferred_element_type=F32))
    y_xa = jnp.concatenate(ys, axis=-1)

    def gate(i):
        z = jnp.dot(xb, wg_ref[:, i * d:(i + 1) * d], preferred_element_type=F32)
        return jax.nn.sigmoid(z + gb_ref[:, i * d:(i + 1) * d])

    merged = (gate(0) * jnp.dot(yna_ref[...], wna_ref[...], preferred_element_type=F32)
              + gate(1) * jnp.dot(y_pool.astype(BF16), wpool_ref[...], preferred_element_type=F32)
              + gate(2) * jnp.dot(y_xa.astype(BF16), wxa_ref[...], preferred_element_type=F32))
    h = x + jnp.dot(merged.astype(BF16), wout_ref[...], preferred_element_type=F32)
    h_out[...] = h
    hn_out[...] = _rms_rows(h, fg_ref[...]).astype(BF16)


def _mix(x2, yna, p, xq, kmem, vmem, g, w_gate, gate_b, pool_w, pool_scale,
         w_na, w_pool, w_xa, w_out, ffn_g, seq):
    n, d = x2.shape
    tm = TOKEN_TILE
    assert seq % tm == 0 and tm % POOL_HALO == 0 and max(POOL_SIZES) // 2 == POOL_HALO
    tiles_per_seq = seq // tm
    halo_blocks = n // POOL_HALO
    per_tile = tm // POOL_HALO
    m = kmem.shape[1]
    na_w, pool_w_, xa_w = yna.shape[1], p.shape[1], xq.shape[1]
    const2 = lambda i: (0, 0)
    const3 = lambda i: (0, 0, 0)
    tile = lambda i: (i, 0)
    mem_idx = lambda i: (i // tiles_per_seq, 0, 0)
    kern = functools.partial(_mix_kernel, seq=seq, tiles_per_seq=tiles_per_seq)
    return pl.pallas_call(
        kern,
        grid=(n // tm,),
        in_specs=[
            pl.BlockSpec((tm, d), tile),
            pl.BlockSpec((tm, na_w), tile),
            pl.BlockSpec((tm, pool_w_), tile),
            pl.BlockSpec((POOL_HALO, pool_w_), lambda i: (jnp.maximum(i * per_tile - 1, 0), 0)),
            pl.BlockSpec((POOL_HALO, pool_w_), lambda i: (jnp.minimum((i + 1) * per_tile, halo_blocks - 1), 0)),
            pl.BlockSpec((tm, xa_w), tile),
            pl.BlockSpec((1, m, xa_w), mem_idx),
            pl.BlockSpec((1, m, xa_w), mem_idx),
            pl.BlockSpec((1, d), const2),
            pl.BlockSpec(w_gate.shape, const2),
            pl.BlockSpec(gate_b.shape, const2),
            pl.BlockSpec(pool_w.shape, const3),
            pl.BlockSpec((1, pool_w_), const2),
            pl.BlockSpec(w_na.shape, const2),
            pl.BlockSpec(w_pool.shape, const2),
            pl.BlockSpec(w_xa.shape, const2),
            pl.BlockSpec(w_out.shape, const2),
            pl.BlockSpec((1, d), const2),
        ],
        out_specs=[pl.BlockSpec((tm, d), tile), pl.BlockSpec((tm, d), tile)],
        out_shape=[jax.ShapeDtypeStruct((n, d), F32), jax.ShapeDtypeStruct((n, d), BF16)],
        scratch_shapes=[pltpu.VMEM((tm + 2 * POOL_HALO, pool_w_), F32)],
        compiler_params=pltpu.CompilerParams(dimension_semantics=("parallel",),
                                             vmem_limit_bytes=VMEM_LIMIT),
        name="mix",
    )(x2, yna, p, p, p, xq, kmem, vmem, g, w_gate, gate_b, pool_w, pool_scale,
      w_na, w_pool, w_xa, w_out, ffn_g)


def _gelu_exact(x):
    return 0.5 * x * (1.0 + lax.erf(x * (2.0 ** -0.5)))


def _candidate_pairs(depth):
    return [(i, j) for i in range(depth) for j in range(depth) if (i + 1) * (j + 1) <= depth]


def _peer_kernel(hn_ref, h_ref, wq_ref, sk_ref, u0_ref, un_ref, vt_ref, o_ref,
                 s2_ref, r2_ref, e2_ref, n_ref, e1_ref, top_ref, at0_ref, at1_ref, wa_ref, acc_ref):
    j = pl.program_id(1)
    tt = hn_ref.shape[0]
    keys = PEER_KEYS
    depth = PEER_TOPK + 1
    eb = un_ref.shape[0]
    groups = eb // keys

    def pre_activations(u_blk):
        return lax.dot_general(u_blk, hn_ref[...], _NT, preferred_element_type=F32)

    @pl.when(j == 0)
    def _thresholds():
        q = jnp.dot(hn_ref[...], wq_ref[...], preferred_element_type=F32).astype(BF16)
        for h in range(PEER_HEADS):
            for half in range(2):
                hp = 2 * h + half
                s = lax.dot_general(sk_ref[hp], q[:, hp * keys:(hp + 1) * keys], _NT,
                                    preferred_element_type=F32)
                if half == 0:
                    e1_ref[h] = s
                else:
                    s2_ref[h] = s
                    rank = jnp.full(s.shape, float(depth), F32)
                for r in range(depth):
                    m = jnp.max(s, axis=0, keepdims=True)
                    top_ref[half, r, h:h + 1, :] = m
                    hit = s >= m
                    if half == 1:
                        rank = jnp.where(hit, float(r), rank)
                    s = jnp.where(hit, -jnp.inf, s)
                if half == 1:
                    r2_ref[h] = rank.astype(BF16)
        a = [top_ref[0, r] for r in range(depth)]
        b = [top_ref[1, r] for r in range(depth)]
        cands = [a[i] + b[k] for i, k in _candidate_pairs(depth)]
        best = []
        for r in range(depth):
            m = functools.reduce(jnp.maximum, cands)
            best.append(m)
            if r + 1 < depth:
                cands = [jnp.where(c >= m, -jnp.inf, c) for c in cands]
        kth, nxt = best[PEER_TOPK - 1], best[PEER_TOPK]
        tau = jnp.where(nxt > -jnp.inf, 0.5 * (kth + nxt), kth)
        z = functools.reduce(jnp.add, [jnp.exp(c - best[0]) for c in best[:PEER_TOPK]])
        top_ref[0, 1] = tau
        top_ref[0, 2] = 1.0 / z
        for h in range(PEER_HEADS):
            s1 = e1_ref[h]
            theta = top_ref[0, 1, h:h + 1, :] - s1
            count = jnp.zeros(s1.shape, F32)
            for r in range(depth):
                count = count + jnp.where(top_ref[1, r, h:h + 1, :] >= theta, 1.0, 0.0)
            n_ref[h] = count
            e1_ref[h] = jnp.exp(s1 - top_ref[0, 0, h:h + 1, :])
            e2 = jnp.exp(s2_ref[h] - top_ref[1, 0, h:h + 1, :]) * top_ref[0, 2, h:h + 1, :]
            e2_ref[h] = e2.astype(BF16)
        acc_ref[...] = jnp.zeros_like(acc_ref)
        at0_ref[...] = pre_activations(u0_ref[...])

    rb = 16
    half_rows = eb // 2

    def step(at_cur, at_next):
        at_next[...] = pre_activations(un_ref[...])
        for l in range(groups):
            i1 = j * groups + l
            row = lambda ref, h: jnp.broadcast_to(ref[h, pl.ds(i1, 1), :], (rb, tt)).astype(BF16)
            cnt = [row(n_ref, h) for h in range(PEER_HEADS)]
            e1 = [row(e1_ref, h) for h in range(PEER_HEADS)]
            for blk in range(keys // rb):
                rows = pl.ds(blk * rb, rb)
                dst = pl.ds(l * keys + blk * rb, rb)
                wt = jnp.zeros((rb, tt), BF16)
                for h in range(PEER_HEADS):
                    sel = r2_ref[h, rows, :] < cnt[h]
                    wt = wt + jnp.where(sel, e2_ref[h, rows, :], jnp.zeros((), BF16)) * e1[h]
                wa_ref[dst, :] = wt * _gelu_exact(at_cur[dst, :]).astype(BF16)
            if (l + 1) * keys == half_rows:
                acc_ref[...] += jnp.dot(vt_ref[:, :half_rows], wa_ref[:half_rows, :],
                                        preferred_element_type=F32)
        acc_ref[...] += jnp.dot(vt_ref[:, half_rows:], wa_ref[half_rows:, :],
                                preferred_element_type=F32)

    @pl.when(j % 2 == 0)
    def _even():
        step(at0_ref, at1_ref)

    @pl.when(j % 2 == 1)
    def _odd():
        step(at1_ref, at0_ref)

    @pl.when(j == pl.num_programs(1) - 1)
    def _finish():
        o_ref[...] = h_ref[...] + acc_ref[...].T


def _peer(hn, h, w_q, sub_keys, u, v_t):
    n, d = hn.shape
    tt = PEER_TOKEN_TILE
    eb = EXPERT_BLOCK
    experts = u.shape[0]
    assert experts == PEER_KEYS * PEER_KEYS and experts % eb == 0 and eb % PEER_KEYS == 0
    qcols = w_q.shape[1]
    assert qcols == PEER_HEADS * 2 * PEER_KEYS, "the query sub-dimension must equal the lane width"
    depth = PEER_TOPK + 1
    return pl.pallas_call(
        _peer_kernel,
        grid=(n // tt, experts // eb),
        in_specs=[
            pl.BlockSpec((tt, d), lambda i, j: (i, 0)),
            pl.BlockSpec((tt, d), lambda i, j: (i, 0)),
            pl.BlockSpec((d, qcols), lambda i, j: (0, 0)),
            pl.BlockSpec(sub_keys.shape, lambda i, j: (0, 0, 0)),
            pl.BlockSpec((eb, d), lambda i, j: (0, 0)),
            pl.BlockSpec((eb, d), lambda i, j: (jnp.minimum(j + 1, experts // eb - 1), 0)),
            pl.BlockSpec((d, eb), lambda i, j: (0, j)),
        ],
        out_specs=pl.BlockSpec((tt, d), lambda i, j: (i, 0)),
        out_shape=jax.ShapeDtypeStruct((n, d), F32),
        scratch_shapes=[
            pltpu.VMEM((PEER_HEADS, PEER_KEYS, tt), F32),
            pltpu.VMEM((PEER_HEADS, PEER_KEYS, tt), BF16),
            pltpu.VMEM((PEER_HEADS, PEER_KEYS, tt), BF16),
            pltpu.VMEM((PEER_HEADS, PEER_KEYS, tt), F32),
            pltpu.VMEM((PEER_HEADS, PEER_KEYS, tt), F32),
            pltpu.VMEM((2, depth, PEER_HEADS, tt), F32),
            pltpu.VMEM((eb, tt), F32),
            pltpu.VMEM((eb, tt), F32),
            pltpu.VMEM((eb, tt), BF16),
            pltpu.VMEM((d, tt), F32),
        ],
        compiler_params=pltpu.CompilerParams(dimension_semantics=("parallel", "arbitrary"),
                                             vmem_limit_bytes=VMEM_LIMIT),
        name="peer",
    )(hn, h, w_q, sub_keys, u, u, v_t)


def kernel(x, mem, mix_norm_g, mem_norm_g, w_in, gate_b, w_mem_kv, na_q_g, na_k_g, na_rpb, pool_w, pool_scale, xa_q_g, xa_k_g, w_branch_na, w_branch_pool, w_branch_xa, w_out, ffn_norm_g, peer_w_q, peer_sub_keys, peer_u, peer_v):
    batch, seq, d = x.shape
    na_width = w_branch_na.shape[1]
    pool_width = w_branch_pool.shape[1]
    xa_width = w_branch_xa.shape[1]
    mix_cols = 3 * na_width + pool_width + xa_width
    n = batch * seq
    assert n % TOKEN_TILE == 0 and n % PEER_TOKEN_TILE == 0 and seq % GRID_W == 0
    row = lambda a: a.reshape(1, -1)
    h = x.reshape(n, d)
    for l in range(mix_norm_g.shape[0]):
        w_in_b = w_in[l].astype(BF16)
        q, k, v, p, xq = _in_proj(
            h, row(mix_norm_g[l]), w_in_b[:, :mix_cols],
            row(jnp.tile(na_q_g[l], NA_HEADS)), row(jnp.tile(na_k_g[l], NA_HEADS)),
            row(jnp.tile(xa_q_g[l], XA_HEADS)), na_width, pool_width, xa_width)
        kmem, vmem = _mem_kv(mem, row(mem_norm_g[l]), w_mem_kv[l].astype(BF16),
                             row(jnp.tile(xa_k_g[l], XA_HEADS)), xa_width)
        y_na = _na_attention(q, k, v, _na_bias_table(na_rpb[l]), batch, seq)
        h, hn = _mix(h, y_na, p, xq, kmem, vmem, row(mix_norm_g[l]), w_in_b[:, mix_cols:],
                     row(gate_b[l]), pool_w[l].astype(BF16), row(pool_scale[l]),
                     w_branch_na[l].astype(BF16), w_branch_pool[l].astype(BF16),
                     w_branch_xa[l].astype(BF16), w_out[l].astype(BF16), row(ffn_norm_g[l]), seq)
        sub_keys = peer_sub_keys[l].reshape(-1, *peer_sub_keys.shape[-2:]).astype(BF16)
        h = _peer(hn, h, peer_w_q[l].astype(BF16), sub_keys,
                  peer_u[l].astype(BF16), peer_v[l].T.astype(BF16))
    return h.reshape(batch, seq, d)
```

```python
import functools

import numpy as np
import jax
import jax.numpy as jnp
from jax import lax
from jax.experimental import pallas as pl
from jax.experimental.pallas import tpu as pltpu

GRID_W = 64
EPS = 1e-6
NEG_INF = -1e30

NA_HEADS = 8
NA_WIN_ROWS = 8
NA_WIN_COLS = 16
NA_ROWS_PER_STEP = 4
POOL_SIZES = (2, 4, 8, 16)
POOL_HALO = 8
XA_HEADS = 4
PEER_HEADS = 8
PEER_KEYS = 128
PEER_TOPK = 16

LANES = 128
TOKEN_TILE = 256
PEER_TOKEN_TILE = 512
EXPERT_BLOCK = 1024
VMEM_LIMIT = 56 * 1024 * 1024

F32 = jnp.float32
BF16 = jnp.bfloat16
_NT = (((1,), (1,)), ((), ()))


def _rms_rows(x, g):
    return x * lax.rsqrt(jnp.mean(x * x, axis=-1, keepdims=True) + EPS) * g


def _group_mean_sq(v, ones_ref, width):
    sq = v * v
    hi = sq.astype(BF16)
    r1 = sq - hi.astype(F32)
    mid = r1.astype(BF16)
    lo = (r1 - mid.astype(F32)).astype(BF16)
    ones = ones_ref[...]
    tot = (jnp.dot(hi, ones, preferred_element_type=F32)
           + jnp.dot(mid, ones, preferred_element_type=F32)
           + jnp.dot(lo, ones, preferred_element_type=F32))
    return tot * (1.0 / width)


def _block_diag_ones(channels, width):
    idx = np.arange(channels) // width
    return jnp.asarray(idx[:, None] == idx[None, :], dtype=BF16)


def _in_proj_kernel(x_ref, g_ref, w_ref, qg_ref, kg_ref, xg_ref, ones_na_ref, ones_xa_ref,
                    q_out, k_out, v_out, p_out, xq_out, *, na_width, pool_width, xa_width,
                    na_head_dim, xa_head_dim):
    xb = _rms_rows(x_ref[...], g_ref[...]).astype(BF16)

    def proj(c0, c1):
        return jnp.dot(xb, w_ref[:, c0:c1], preferred_element_type=F32)

    c = 0
    q = proj(c, c + na_width); c += na_width
    k = proj(c, c + na_width); c += na_width
    v = proj(c, c + na_width); c += na_width
    p = proj(c, c + pool_width); c += pool_width
    xq = proj(c, c + xa_width)
    qn = q * lax.rsqrt(_group_mean_sq(q, ones_na_ref, na_head_dim) + EPS) * qg_ref[...]
    q_out[...] = (qn * (na_head_dim ** -0.5)).astype(BF16)
    kn = k * lax.rsqrt(_group_mean_sq(k, ones_na_ref, na_head_dim) + EPS) * kg_ref[...]
    k_out[...] = kn.astype(BF16)
    v_out[...] = v.astype(BF16)
    p_out[...] = p
    xqn = xq * lax.rsqrt(_group_mean_sq(xq, ones_xa_ref, xa_head_dim) + EPS) * xg_ref[...]
    xq_out[...] = xqn.astype(BF16)


def _in_proj(x2, g, w_cols, qg, kg, xg, na_width, pool_width, xa_width):
    n, d = x2.shape
    tm = TOKEN_TILE
    na_hd = na_width // NA_HEADS
    xa_hd = xa_width // XA_HEADS
    assert na_hd in (4, 16, 64), "the folded attention scale must be a power of two"
    cols = w_cols.shape[1]
    const = lambda i: (0, 0)
    tile = lambda i: (i, 0)
    kern = functools.partial(_in_proj_kernel, na_width=na_width, pool_width=pool_width,
                             xa_width=xa_width, na_head_dim=na_hd, xa_head_dim=xa_hd)
    return pl.pallas_call(
        kern,
        grid=(n // tm,),
        in_specs=[
            pl.BlockSpec((tm, d), tile),
            pl.BlockSpec((1, d), const),
            pl.BlockSpec((d, cols), const),
            pl.BlockSpec((1, na_width), const),
            pl.BlockSpec((1, na_width), const),
            pl.BlockSpec((1, xa_width), const),
            pl.BlockSpec((na_width, na_width), const),
            pl.BlockSpec((xa_width, xa_width), const),
        ],
        out_specs=[
            pl.BlockSpec((tm, na_width), tile),
            pl.BlockSpec((tm, na_width), tile),
            pl.BlockSpec((tm, na_width), tile),
            pl.BlockSpec((tm, pool_width), tile),
            pl.BlockSpec((tm, xa_width), tile),
        ],
        out_shape=[
            jax.ShapeDtypeStruct((n, na_width), BF16),
            jax.ShapeDtypeStruct((n, na_width), BF16),
            jax.ShapeDtypeStruct((n, na_width), BF16),
            jax.ShapeDtypeStruct((n, pool_width), F32),
            jax.ShapeDtypeStruct((n, xa_width), BF16),
        ],
        compiler_params=pltpu.CompilerParams(dimension_semantics=("parallel",),
                                             vmem_limit_bytes=VMEM_LIMIT),
        name="in_proj",
    )(x2, g, w_cols, qg, kg, xg, _block_diag_ones(na_width, na_hd), _block_diag_ones(xa_width, xa_hd))


def _mem_kv_kernel(mem_ref, g_ref, w_ref, kg_ref, ones_ref, k_out, v_out, *, xa_width, xa_head_dim):
    mn = _rms_rows(mem_ref[0], g_ref[...]).astype(BF16)
    kv = jnp.dot(mn, w_ref[...], preferred_element_type=F32)
    k = kv[:, :xa_width]
    kn = k * lax.rsqrt(_group_mean_sq(k, ones_ref, xa_head_dim) + EPS) * kg_ref[...]
    k_out[0] = kn.astype(BF16)
    v_out[0] = kv[:, xa_width:].astype(BF16)


def _mem_kv(mem, g, w_kv, kg, xa_width):
    b, m, d = mem.shape
    xa_hd = xa_width // XA_HEADS
    const = lambda i: (0, 0)
    kern = functools.partial(_mem_kv_kernel, xa_width=xa_width, xa_head_dim=xa_hd)
    return pl.pallas_call(
        kern,
        grid=(b,),
        in_specs=[
            pl.BlockSpec((1, m, d), lambda i: (i, 0, 0)),
            pl.BlockSpec((1, d), const),
            pl.BlockSpec((d, 2 * xa_width), const),
            pl.BlockSpec((1, xa_width), const),
            pl.BlockSpec((xa_width, xa_width), const),
        ],
        out_specs=[pl.BlockSpec((1, m, xa_width), lambda i: (i, 0, 0))] * 2,
        out_shape=[jax.ShapeDtypeStruct((b, m, xa_width), BF16)] * 2,
        compiler_params=pltpu.CompilerParams(dimension_semantics=("parallel",),
                                             vmem_limit_bytes=VMEM_LIMIT),
        name="mem_kv",
    )(mem, g, w_kv, kg, _block_diag_ones(xa_width, xa_hd))


def _na_bias_table(rpb):
    wr, w = NA_WIN_ROWS, GRID_W
    qc = np.arange(w)[:, None]
    kc = np.arange(w)[None, :]
    cs = np.clip(qc - NA_WIN_COLS // 2, 0, w - NA_WIN_COLS)
    col_ok = (kc >= cs) & (kc < cs + NA_WIN_COLS)
    pad = w - NA_WIN_COLS
    ext = jnp.pad(rpb.astype(F32), ((0, 0), (0, 0), (pad, pad)), mode="edge")
    by_q = jnp.stack([ext[:, :, w - 1 - c:2 * w - 1 - c] for c in range(w)], axis=2)
    by_q = jnp.where(col_ok[None, None], by_q, NEG_INF)
    tab = jnp.stack([by_q[:, wr - 1 - d:2 * wr - 1 - d] for d in range(wr)], axis=0)
    tab = tab.transpose(0, 1, 3, 2, 4)
    return tab.reshape(wr, rpb.shape[0], w, wr * w)


def _na_union_bias(bias_tab, rows):
    rb, wr, w = NA_ROWS_PER_STEP, NA_WIN_ROWS, GRID_W
    union = wr + rb
    heads = bias_tab.shape[1]
    old = bias_tab.reshape(wr, heads, w, wr, w)
    neg = lambda n: jnp.full((heads, w, n, w), NEG_INF, F32)
    first = [(rq, 0) for rq in range(rb)]
    interior = [(rb, rq) for rq in range(rb)]
    last = [(rb + rq, rb) for rq in range(rb)]
    types = []
    for pattern in (first, interior, last):
        per_row = [jnp.concatenate([neg(off), old[d], neg(union - wr - off)], axis=2)
                   for d, off in pattern]
        t = jnp.stack(per_row, axis=1)
        t = t.reshape(heads // 2, 2 * rb * w, union * w)
        types.append(t)
    return jnp.stack(types)


def _na_kernel(q_ref, k_ref, v_ref, bias_ref, o_ref, *, rows):
    blk = pl.program_id(1)
    rb = NA_ROWS_PER_STEP
    union = (NA_WIN_ROWS + rb) * GRID_W
    ks = jnp.clip(blk * rb - NA_WIN_ROWS // 2, 0, rows - NA_WIN_ROWS - rb)
    start = pl.multiple_of(ks * GRID_W, GRID_W)
    head_dim = q_ref.shape[-1] // NA_HEADS
    tokens = rb * GRID_W
    low = lax.broadcasted_iota(jnp.int32, (tokens, LANES), 1) < head_dim
    outs = []
    for t in range(q_ref.shape[-1] // LANES):
        cols = slice(t * LANES, (t + 1) * LANES)
        qt = q_ref[0, :, cols]
        kt = k_ref[0, pl.ds(start, union), cols]
        vt = v_ref[0, pl.ds(start, union), cols]
        zero = jnp.zeros_like(qt)
        q2 = jnp.concatenate([jnp.where(low, qt, zero), jnp.where(low, zero, qt)], axis=0)
        s = lax.dot_general(q2, kt, _NT, preferred_element_type=F32) + bias_ref[0, t]
        e = jnp.exp(s - jnp.max(s, axis=-1, keepdims=True))
        p = e / jnp.sum(e, axis=-1, keepdims=True)
        o2 = jnp.dot(p.astype(BF16), vt, preferred_element_type=F32)
        outs.append(jnp.where(low, o2[:tokens], o2[tokens:]))
    o_ref[0] = jnp.concatenate(outs, axis=-1).astype(BF16)


def _na_attention(q, k, v, bias_tab, batch, seq):
    width = q.shape[-1]
    rows = seq // GRID_W
    rb = NA_ROWS_PER_STEP
    assert LANES == 2 * (width // NA_HEADS), "two heads per lane tile"
    assert rb == NA_WIN_ROWS // 2 and rows % rb == 0 and rows >= NA_WIN_ROWS + 2 * rb
    nblk = rows // rb
    q3, k3, v3 = (a.reshape(batch, seq, width) for a in (q, k, v))
    tab = _na_union_bias(bias_tab, rows)

    def bias_idx(b, i):
        return (jnp.where(i == 0, 0, jnp.where(i == nblk - 1, 2, 1)), 0, 0, 0)

    full = lambda b, i: (b, 0, 0)
    once = pl.Buffered(1)
    out = pl.pallas_call(
        functools.partial(_na_kernel, rows=rows),
        grid=(batch, nblk),
        in_specs=[
            pl.BlockSpec((1, rb * GRID_W, width), lambda b, i: (b, i, 0)),
            pl.BlockSpec((1, seq, width), full, pipeline_mode=once),
            pl.BlockSpec((1, seq, width), full, pipeline_mode=once),
            pl.BlockSpec((1,) + tab.shape[1:], bias_idx, pipeline_mode=once),
        ],
        out_specs=pl.BlockSpec((1, rb * GRID_W, width), lambda b, i: (b, i, 0)),
        out_shape=jax.ShapeDtypeStruct((batch, seq, width), BF16),
        compiler_params=pltpu.CompilerParams(dimension_semantics=("parallel", "arbitrary"),
                                             vmem_limit_bytes=VMEM_LIMIT),
        name="na_attn",
    )(q3, k3, v3, tab)
    return out.reshape(batch * seq, width)


def _mix_kernel(x_ref, yna_ref, p_ref, pprev_ref, pnext_ref, xq_ref, km_ref, vm_ref,
                g_ref, wg_ref, gb_ref, pw_ref, ps_ref, wna_ref, wpool_ref, wxa_ref, wout_ref, fg_ref,
                h_out, hn_out, pbuf, *, seq, tiles_per_seq):
    tm, d = x_ref.shape
    ti = pl.program_id(0) % tiles_per_seq
    x = x_ref[...]
    xb = _rms_rows(x, g_ref[...]).astype(BF16)

    halo = POOL_HALO
    p = p_ref[...]
    pbuf[0:halo, :] = jnp.where(ti == 0, 0.0, pprev_ref[...])
    pbuf[halo:halo + tm, :] = p
    pbuf[halo + tm:2 * halo + tm, :] = jnp.where(ti == tiles_per_seq - 1, 0.0, pnext_ref[...])
    pos = ti * tm + lax.broadcasted_iota(jnp.int32, (tm, 1), 0)
    group = p.shape[-1] // len(POOL_SIZES)
    mixed = []
    for gi, w in enumerate(POOL_SIZES):
        cols = slice(gi * group, (gi + 1) * group)
        tot = pbuf[halo - w // 2:halo - w // 2 + tm, cols]
        for j in range(1 - w // 2, w // 2):
            tot = tot + pbuf[halo + j:halo + j + tm, cols]
        cnt = (jnp.minimum(pos + w // 2, seq) - jnp.maximum(pos - w // 2, 0)).astype(F32)
        pooled = tot / cnt - p[:, cols]
        mixed.append(jnp.dot(pooled.astype(BF16), pw_ref[gi], preferred_element_type=F32))
    y_pool = jnp.concatenate(mixed, axis=-1) * ps_ref[...]

    xq = xq_ref[...]
    xa_hd = xq.shape[-1] // XA_HEADS
    ys = []
    for h in range(XA_HEADS):
        cols = slice(h * xa_hd, (h + 1) * xa_hd)
        s = lax.dot_general(xq[:, cols], km_ref[0, :, cols], _NT, preferred_element_type=F32)
        s = s * (xa_hd ** -0.5)
        e = jnp.exp(s - jnp.max(s, axis=-1, keepdims=True))
        pr = e / jnp.sum(e, axis=-1, keepdims=True)
        ys.append(jnp.dot(pr.astype(BF16), vm_ref[0, :, cols], preferred_element_type=F32))
    y_xa = jnp.concatenate(ys, axis=-1)

    def gate(i):
        z = jnp.dot(xb, wg_ref[:, i * d:(i + 1) * d], preferred_element_type=F32)
        return jax.nn.sigmoid(z + gb_ref[:, i * d:(i + 1) * d])

    merged = (gate(0) * jnp.dot(yna_ref[...], wna_ref[...], preferred_element_type=F32)
              + gate(1) * jnp.dot(y_pool.astype(BF16), wpool_ref[...], preferred_element_type=F32)
              + gate(2) * jnp.dot(y_xa.astype(BF16), wxa_ref[...], preferred_element_type=F32))
    h = x + jnp.dot(merged.astype(BF16), wout_ref[...], preferred_element_type=F32)
    h_out[...] = h
    hn_out[...] = _rms_rows(h, fg_ref[...]).astype(BF16)


def _mix(x2, yna, p, xq, kmem, vmem, g, w_gate, gate_b, pool_w, pool_scale,
         w_na, w_pool, w_xa, w_out, ffn_g, seq):
    n, d = x2.shape
    tm = TOKEN_TILE
    assert seq % tm == 0 and tm % POOL_HALO == 0 and max(POOL_SIZES) // 2 == POOL_HALO
    tiles_per_seq = seq // tm
    halo_blocks = n // POOL_HALO
    per_tile = tm // POOL_HALO
    m = kmem.shape[1]
    na_w, pool_w_, xa_w = yna.shape[1], p.shape[1], xq.shape[1]
    const2 = lambda i: (0, 0)
    const3 = lambda i: (0, 0, 0)
    tile = lambda i: (i, 0)
    mem_idx = lambda i: (i // tiles_per_seq, 0, 0)
    kern = functools.partial(_mix_kernel, seq=seq, tiles_per_seq=tiles_per_seq)
    return pl.pallas_call(
        kern,
        grid=(n // tm,),
        in_specs=[
            pl.BlockSpec((tm, d), tile),
            pl.BlockSpec((tm, na_w), tile),
            pl.BlockSpec((tm, pool_w_), tile),
            pl.BlockSpec((POOL_HALO, pool_w_), lambda i: (jnp.maximum(i * per_tile - 1, 0), 0)),
            pl.BlockSpec((POOL_HALO, pool_w_), lambda i: (jnp.minimum((i + 1) * per_tile, halo_blocks - 1), 0)),
            pl.BlockSpec((tm, xa_w), tile),
            pl.BlockSpec((1, m, xa_w), mem_idx),
            pl.BlockSpec((1, m, xa_w), mem_idx),
            pl.BlockSpec((1, d), const2),
            pl.BlockSpec(w_gate.shape, const2),
            pl.BlockSpec(gate_b.shape, const2),
            pl.BlockSpec(pool_w.shape, const3),
            pl.BlockSpec((1, pool_w_), const2),
            pl.BlockSpec(w_na.shape, const2),
            pl.BlockSpec(w_pool.shape, const2),
            pl.BlockSpec(w_xa.shape, const2),
            pl.BlockSpec(w_out.shape, const2),
            pl.BlockSpec((1, d), const2),
        ],
        out_specs=[pl.BlockSpec((tm, d), tile), pl.BlockSpec((tm, d), tile)],
        out_shape=[jax.ShapeDtypeStruct((n, d), F32), jax.ShapeDtypeStruct((n, d), BF16)],
        scratch_shapes=[pltpu.VMEM((tm + 2 * POOL_HALO, pool_w_), F32)],
        compiler_params=pltpu.CompilerParams(dimension_semantics=("parallel",),
                                             vmem_limit_bytes=VMEM_LIMIT),
        name="mix",
    )(x2, yna, p, p, p, xq, kmem, vmem, g, w_gate, gate_b, pool_w, pool_scale,
      w_na, w_pool, w_xa, w_out, ffn_g)


def _gelu_exact(x):
    return 0.5 * x * (1.0 + lax.erf(x * (2.0 ** -0.5)))


def _candidate_pairs(depth):
    return [(i, j) for i in range(depth) for j in range(depth) if (i + 1) * (j + 1) <= depth]


def _peer_kernel(hn_ref, h_ref, wq_ref, sk_ref, u0_ref, un_ref, vt_ref, o_ref,
                 s2_ref, r2_ref, e2_ref, n_ref, e1_ref, top_ref, at0_ref, at1_ref, wa_ref, acc_ref):
    j = pl.program_id(1)
    tt = hn_ref.shape[0]
    keys = PEER_KEYS
    depth = PEER_TOPK + 1
    eb = un_ref.shape[0]
    groups = eb // keys

    def pre_activations(u_blk):
        return lax.dot_general(u_blk, hn_ref[...], _NT, preferred_element_type=F32)

    @pl.when(j == 0)
    def _thresholds():
        lane_tiles = [slice(c * LANES, (c + 1) * LANES) for c in range(tt // LANES)]
        q = jnp.dot(hn_ref[...], wq_ref[...], preferred_element_type=F32).astype(BF16)
        for h in range(PEER_HEADS):
            for half in range(2):
                hp = 2 * h + half
                s = lax.dot_general(sk_ref[hp], q[:, hp * keys:(hp + 1) * keys], _NT,
                                    preferred_element_type=F32)
                if half == 0:
                    e1_ref[h] = s
                else:
                    s2_ref[h] = s
                for cols in lane_tiles:
                    x = s[:, cols]
                    rank = jnp.full(x.shape, float(depth), F32)
                    for r in range(depth):
                        m = jnp.max(x, axis=0, keepdims=True)
                        top_ref[half, r, h:h + 1, cols] = m
                        hit = x >= m
                        if half == 1:
                            rank = jnp.where(hit, float(r), rank)
                        x = jnp.where(hit, -jnp.inf, x)
                    if half == 1:
                        r2_ref[h, :, cols] = rank.astype(BF16)
        for cols in lane_tiles:
            a = [top_ref[0, r, :, cols] for r in range(depth)]
            b = [top_ref[1, r, :, cols] for r in range(depth)]
            cands = [a[i] + b[k] for i, k in _candidate_pairs(depth)]
            best = []
            for r in range(depth):
                m = functools.reduce(jnp.maximum, cands)
                best.append(m)
                if r + 1 < depth:
                    cands = [jnp.where(c >= m, -jnp.inf, c) for c in cands]
            kth, nxt = best[PEER_TOPK - 1], best[PEER_TOPK]
            tau = jnp.where(nxt > -jnp.inf, 0.5 * (kth + nxt), kth)
            z = functools.reduce(jnp.add, [jnp.exp(c - best[0]) for c in best[:PEER_TOPK]])
            zinv = 1.0 / z
            for h in range(PEER_HEADS):
                s1 = e1_ref[h, :, cols]
                theta = tau[h:h + 1] - s1
                count = jnp.zeros(s1.shape, F32)
                for r in range(depth):
                    count = jnp.where(b[r][h:h + 1] >= theta, float(r + 1), count)
                n_ref[h, :, cols] = count
                e1_ref[h, :, cols] = jnp.exp(s1 - a[0][h:h + 1])
                e2 = jnp.exp(s2_ref[h, :, cols] - b[0][h:h + 1]) * zinv[h:h + 1]
                e2_ref[h, :, cols] = e2.astype(BF16)
        acc_ref[...] = jnp.zeros_like(acc_ref)
        at0_ref[...] = pre_activations(u0_ref[...])

    rb = 16
    half_rows = eb // 2

    def step(at_cur, at_next):
        at_next[...] = pre_activations(un_ref[...])
        base = pl.multiple_of(j * groups, groups)
        n_tile = [n_ref[h, pl.ds(base, groups), :] for h in range(PEER_HEADS)]
        e1_tile = [e1_ref[h, pl.ds(base, groups), :] for h in range(PEER_HEADS)]
        for l in range(groups):
            row = lambda tile: jnp.broadcast_to(tile[l:l + 1, :], (rb, tt)).astype(BF16)
            cnt = [row(n_tile[h]) for h in range(PEER_HEADS)]
            e1 = [row(e1_tile[h]) for h in range(PEER_HEADS)]
            for blk in range(keys // rb):
                rows = pl.ds(blk * rb, rb)
                dst = pl.ds(l * keys + blk * rb, rb)
                wt = jnp.zeros((rb, tt), BF16)
                for h in range(PEER_HEADS):
                    sel = r2_ref[h, rows, :] < cnt[h]
                    wt = wt + jnp.where(sel, e2_ref[h, rows, :], jnp.zeros((), BF16)) * e1[h]
                wa_ref[dst, :] = wt * _gelu_exact(at_cur[dst, :]).astype(BF16)
            if (l + 1) * keys == half_rows:
                acc_ref[...] += jnp.dot(vt_ref[0, :, :half_rows], wa_ref[:half_rows, :],
                                        preferred_element_type=F32)
        acc_ref[...] += jnp.dot(vt_ref[0, :, half_rows:], wa_ref[half_rows:, :],
                                preferred_element_type=F32)

    @pl.when(j % 2 == 0)
    def _even():
        step(at0_ref, at1_ref)

    @pl.when(j % 2 == 1)
    def _odd():
        step(at1_ref, at0_ref)

    @pl.when(j == pl.num_programs(1) - 1)
    def _finish():
        o_ref[...] = h_ref[...] + acc_ref[...].T


def _peer(hn, h, w_q, sub_keys, u, v):
    n, d = hn.shape
    tt = PEER_TOKEN_TILE
    eb = EXPERT_BLOCK
    experts = u.shape[0]
    assert experts == PEER_KEYS * PEER_KEYS and experts % eb == 0 and eb % PEER_KEYS == 0
    qcols = w_q.shape[1]
    assert qcols == PEER_HEADS * 2 * PEER_KEYS, "the query sub-dimension must equal the lane width"
    depth = PEER_TOPK + 1
    v_t = v.reshape(experts // eb, eb, d).transpose(0, 2, 1)
    return pl.pallas_call(
        _peer_kernel,
        grid=(n // tt, experts // eb),
        in_specs=[
            pl.BlockSpec((tt, d), lambda i, j: (i, 0)),
            pl.BlockSpec((tt, d), lambda i, j: (i, 0)),
            pl.BlockSpec((d, qcols), lambda i, j: (0, 0)),
            pl.BlockSpec(sub_keys.shape, lambda i, j: (0, 0, 0)),
            pl.BlockSpec((eb, d), lambda i, j: (0, 0)),
            pl.BlockSpec((eb, d), lambda i, j: (jnp.minimum(j + 1, experts // eb - 1), 0)),
            pl.BlockSpec((1, d, eb), lambda i, j: (j, 0, 0)),
        ],
        out_specs=pl.BlockSpec((tt, d), lambda i, j: (i, 0)),
        out_shape=jax.ShapeDtypeStruct((n, d), F32),
        scratch_shapes=[
            pltpu.VMEM((PEER_HEADS, PEER_KEYS, tt), F32),
            pltpu.VMEM((PEER_HEADS, PEER_KEYS, tt), BF16),
            pltpu.VMEM((PEER_HEADS, PEER_KEYS, tt), BF16),
            pltpu.VMEM((PEER_HEADS, PEER_KEYS, tt), F32),
            pltpu.VMEM((PEER_HEADS, PEER_KEYS, tt), F32),
            pltpu.VMEM((2, depth, PEER_HEADS, tt), F32),
            pltpu.VMEM((eb, tt), F32),
            pltpu.VMEM((eb, tt), F32),
            pltpu.VMEM((eb, tt), BF16),
            pltpu.VMEM((d, tt), F32),
        ],
        compiler_params=pltpu.CompilerParams(dimension_semantics=("parallel", "arbitrary"),
                                             vmem_limit_bytes=VMEM_LIMIT),
        name="peer",
    )(hn, h, w_q, sub_keys, u, u, v_t)


def kernel(x, mem, mix_norm_g, mem_norm_g, w_in, gate_b, w_mem_kv, na_q_g, na_k_g, na_rpb, pool_w, pool_scale, xa_q_g, xa_k_g, w_branch_na, w_branch_pool, w_branch_xa, w_out, ffn_norm_g, peer_w_q, peer_sub_keys, peer_u, peer_v):
    batch, seq, d = x.shape
    na_width = w_branch_na.shape[1]
    pool_width = w_branch_pool.shape[1]
    xa_width = w_branch_xa.shape[1]
    mix_cols = 3 * na_width + pool_width + xa_width
    n = batch * seq
    assert n % TOKEN_TILE == 0 and n % PEER_TOKEN_TILE == 0 and seq % GRID_W == 0
    row = lambda a: a.reshape(1, -1)
    h = x.reshape(n, d)
    for l in range(mix_norm_g.shape[0]):
        w_in_b = w_in[l].astype(BF16)
        q, k, v, p, xq = _in_proj(
            h, row(mix_norm_g[l]), w_in_b[:, :mix_cols],
            row(jnp.tile(na_q_g[l], NA_HEADS)), row(jnp.tile(na_k_g[l], NA_HEADS)),
            row(jnp.tile(xa_q_g[l], XA_HEADS)), na_width, pool_width, xa_width)
        kmem, vmem = _mem_kv(mem, row(mem_norm_g[l]), w_mem_kv[l].astype(BF16),
                             row(jnp.tile(xa_k_g[l], XA_HEADS)), xa_width)
        y_na = _na_attention(q, k, v, _na_bias_table(na_rpb[l]), batch, seq)
        h, hn = _mix(h, y_na, p, xq, kmem, vmem, row(mix_norm_g[l]), w_in_b[:, mix_cols:],
                     row(gate_b[l]), pool_w[l].astype(BF16), row(pool_scale[l]),
                     w_branch_na[l].astype(BF16), w_branch_pool[l].astype(BF16),
                     w_branch_xa[l].astype(BF16), w_out[l].astype(BF16), row(ffn_norm_g[l]), seq)
        sub_keys = peer_sub_keys[l].reshape(-1, *peer_sub_keys.shape[-2:]).astype(BF16)
        h = _peer(hn, h, peer_w_q[l].astype(BF16), sub_keys,
                  peer_u[l].astype(BF16), peer_v[l].astype(BF16))
    return h.reshape(batch, seq, d)
```

```python
import functools

import numpy as np
import jax
import jax.numpy as jnp
from jax import lax
from jax.experimental import pallas as pl
from jax.experimental.pallas import tpu as pltpu

GRID_W = 64
EPS = 1e-6
NEG_INF = -1e30

NA_HEADS = 8
NA_WIN_ROWS = 8
NA_WIN_COLS = 16
NA_ROWS_PER_STEP = 4
POOL_SIZES = (2, 4, 8, 16)
POOL_HALO = 8
XA_HEADS = 4
PEER_HEADS = 8
PEER_KEYS = 128
PEER_TOPK = 16

LANES = 128
TOKEN_TILE = 256
PEER_TOKEN_TILE = 512
EXPERT_BLOCK = 1024
VMEM_LIMIT = 56 * 1024 * 1024

F32 = jnp.float32
BF16 = jnp.bfloat16
_NT = (((1,), (1,)), ((), ()))


def _rms_rows(x, g):
    return x * lax.rsqrt(jnp.mean(x * x, axis=-1, keepdims=True) + EPS) * g


def _group_mean_sq(v, ones_ref, width):
    sq = v * v
    hi = sq.astype(BF16)
    r1 = sq - hi.astype(F32)
    mid = r1.astype(BF16)
    lo = (r1 - mid.astype(F32)).astype(BF16)
    ones = ones_ref[...]
    tot = (jnp.dot(hi, ones, preferred_element_type=F32)
           + jnp.dot(mid, ones, preferred_element_type=F32)
           + jnp.dot(lo, ones, preferred_element_type=F32))
    return tot * (1.0 / width)


def _block_diag_ones(channels, width):
    idx = np.arange(channels) // width
    return jnp.asarray(idx[:, None] == idx[None, :], dtype=BF16)


def _in_proj_kernel(x_ref, g_ref, w_ref, qg_ref, kg_ref, xg_ref, ones_na_ref, ones_xa_ref,
                    q_out, k_out, v_out, p_out, xq_out, *, na_width, pool_width, xa_width,
                    na_head_dim, xa_head_dim):
    xb = _rms_rows(x_ref[...], g_ref[...]).astype(BF16)

    def proj(c0, c1):
        return jnp.dot(xb, w_ref[:, c0:c1], preferred_element_type=F32)

    c = 0
    q = proj(c, c + na_width); c += na_width
    k = proj(c, c + na_width); c += na_width
    v = proj(c, c + na_width); c += na_width
    p = proj(c, c + pool_width); c += pool_width
    xq = proj(c, c + xa_width)
    qn = q * lax.rsqrt(_group_mean_sq(q, ones_na_ref, na_head_dim) + EPS) * qg_ref[...]
    q_out[...] = (qn * (na_head_dim ** -0.5)).astype(BF16)
    kn = k * lax.rsqrt(_group_mean_sq(k, ones_na_ref, na_head_dim) + EPS) * kg_ref[...]
    k_out[...] = kn.astype(BF16)
    v_out[...] = v.astype(BF16)
    p_out[...] = p
    xqn = xq * lax.rsqrt(_group_mean_sq(xq, ones_xa_ref, xa_head_dim) + EPS) * xg_ref[...]
    xq_out[...] = xqn.astype(BF16)


def _in_proj(x2, g, w_cols, qg, kg, xg, na_width, pool_width, xa_width):
    n, d = x2.shape
    tm = TOKEN_TILE
    na_hd = na_width // NA_HEADS
    xa_hd = xa_width // XA_HEADS
    assert na_hd in (4, 16, 64), "the folded attention scale must be a power of two"
    cols = w_cols.shape[1]
    const = lambda i: (0, 0)
    tile = lambda i: (i, 0)
    kern = functools.partial(_in_proj_kernel, na_width=na_width, pool_width=pool_width,
                             xa_width=xa_width, na_head_dim=na_hd, xa_head_dim=xa_hd)
    return pl.pallas_call(
        kern,
        grid=(n // tm,),
        in_specs=[
            pl.BlockSpec((tm, d), tile),
            pl.BlockSpec((1, d), const),
            pl.BlockSpec((d, cols), const),
            pl.BlockSpec((1, na_width), const),
            pl.BlockSpec((1, na_width), const),
            pl.BlockSpec((1, xa_width), const),
            pl.BlockSpec((na_width, na_width), const),
            pl.BlockSpec((xa_width, xa_width), const),
        ],
        out_specs=[
            pl.BlockSpec((tm, na_width), tile),
            pl.BlockSpec((tm, na_width), tile),
            pl.BlockSpec((tm, na_width), tile),
            pl.BlockSpec((tm, pool_width), tile),
            pl.BlockSpec((tm, xa_width), tile),
        ],
        out_shape=[
            jax.ShapeDtypeStruct((n, na_width), BF16),
            jax.ShapeDtypeStruct((n, na_width), BF16),
            jax.ShapeDtypeStruct((n, na_width), BF16),
            jax.ShapeDtypeStruct((n, pool_width), F32),
            jax.ShapeDtypeStruct((n, xa_width), BF16),
        ],
        compiler_params=pltpu.CompilerParams(dimension_semantics=("parallel",),
                                             vmem_limit_bytes=VMEM_LIMIT),
        name="in_proj",
    )(x2, g, w_cols, qg, kg, xg, _block_diag_ones(na_width, na_hd), _block_diag_ones(xa_width, xa_hd))


def _mem_kv_kernel(mem_ref, g_ref, w_ref, kg_ref, ones_ref, k_out, v_out, *, xa_width, xa_head_dim):
    mn = _rms_rows(mem_ref[0], g_ref[...]).astype(BF16)
    kv = jnp.dot(mn, w_ref[...], preferred_element_type=F32)
    k = kv[:, :xa_width]
    kn = k * lax.rsqrt(_group_mean_sq(k, ones_ref, xa_head_dim) + EPS) * kg_ref[...]
    k_out[0] = kn.astype(BF16)
    v_out[0] = kv[:, xa_width:].astype(BF16)


def _mem_kv(mem, g, w_kv, kg, xa_width):
    b, m, d = mem.shape
    xa_hd = xa_width // XA_HEADS
    const = lambda i: (0, 0)
    kern = functools.partial(_mem_kv_kernel, xa_width=xa_width, xa_head_dim=xa_hd)
    return pl.pallas_call(
        kern,
        grid=(b,),
        in_specs=[
            pl.BlockSpec((1, m, d), lambda i: (i, 0, 0)),
            pl.BlockSpec((1, d), const),
            pl.BlockSpec((d, 2 * xa_width), const),
            pl.BlockSpec((1, xa_width), const),
            pl.BlockSpec((xa_width, xa_width), const),
        ],
        out_specs=[pl.BlockSpec((1, m, xa_width), lambda i: (i, 0, 0))] * 2,
        out_shape=[jax.ShapeDtypeStruct((b, m, xa_width), BF16)] * 2,
        compiler_params=pltpu.CompilerParams(dimension_semantics=("parallel",),
                                             vmem_limit_bytes=VMEM_LIMIT),
        name="mem_kv",
    )(mem, g, w_kv, kg, _block_diag_ones(xa_width, xa_hd))


def _na_bias_table(rpb):
    wr, w = NA_WIN_ROWS, GRID_W
    qc = np.arange(w)[:, None]
    kc = np.arange(w)[None, :]
    cs = np.clip(qc - NA_WIN_COLS // 2, 0, w - NA_WIN_COLS)
    col_ok = (kc >= cs) & (kc < cs + NA_WIN_COLS)
    pad = w - NA_WIN_COLS
    ext = jnp.pad(rpb.astype(F32), ((0, 0), (0, 0), (pad, pad)), mode="edge")
    by_q = jnp.stack([ext[:, :, w - 1 - c:2 * w - 1 - c] for c in range(w)], axis=2)
    by_q = jnp.where(col_ok[None, None], by_q, NEG_INF)
    tab = jnp.stack([by_q[:, wr - 1 - d:2 * wr - 1 - d] for d in range(wr)], axis=0)
    tab = tab.transpose(0, 1, 3, 2, 4)
    return tab.reshape(wr, rpb.shape[0], w, wr * w)


def _na_union_bias(bias_tab, rows):
    rb, wr, w = NA_ROWS_PER_STEP, NA_WIN_ROWS, GRID_W
    union = wr + rb
    heads = bias_tab.shape[1]
    old = bias_tab.reshape(wr, heads, w, wr, w)
    neg = lambda n: jnp.full((heads, w, n, w), NEG_INF, F32)
    first = [(rq, 0) for rq in range(rb)]
    interior = [(rb, rq) for rq in range(rb)]
    last = [(rb + rq, rb) for rq in range(rb)]
    types = []
    for pattern in (first, interior, last):
        per_row = [jnp.concatenate([neg(off), old[d], neg(union - wr - off)], axis=2)
                   for d, off in pattern]
        t = jnp.stack(per_row, axis=1)
        t = t.reshape(heads // 2, 2 * rb * w, union * w)
        types.append(t)
    return jnp.stack(types)


def _na_kernel(q_ref, k_ref, v_ref, bias_ref, o_ref, *, rows):
    blk = pl.program_id(1)
    rb = NA_ROWS_PER_STEP
    union = (NA_WIN_ROWS + rb) * GRID_W
    ks = jnp.clip(blk * rb - NA_WIN_ROWS // 2, 0, rows - NA_WIN_ROWS - rb)
    start = pl.multiple_of(ks * GRID_W, GRID_W)
    head_dim = q_ref.shape[-1] // NA_HEADS
    tokens = rb * GRID_W
    low = lax.broadcasted_iota(jnp.int32, (tokens, LANES), 1) < head_dim
    outs = []
    for t in range(q_ref.shape[-1] // LANES):
        cols = slice(t * LANES, (t + 1) * LANES)
        qt = q_ref[0, :, cols]
        kt = k_ref[0, pl.ds(start, union), cols]
        vt = v_ref[0, pl.ds(start, union), cols]
        zero = jnp.zeros_like(qt)
        q2 = jnp.concatenate([jnp.where(low, qt, zero), jnp.where(low, zero, qt)], axis=0)
        s = lax.dot_general(q2, kt, _NT, preferred_element_type=F32) + bias_ref[0, t]
        e = jnp.exp(s - jnp.max(s, axis=-1, keepdims=True))
        p = e / jnp.sum(e, axis=-1, keepdims=True)
        o2 = jnp.dot(p.astype(BF16), vt, preferred_element_type=F32)
        outs.append(jnp.where(low, o2[:tokens], o2[tokens:]))
    o_ref[0] = jnp.concatenate(outs, axis=-1).astype(BF16)


def _na_attention(q, k, v, bias_tab, batch, seq):
    width = q.shape[-1]
    rows = seq // GRID_W
    rb = NA_ROWS_PER_STEP
    assert LANES == 2 * (width // NA_HEADS), "two heads per lane tile"
    assert rb == NA_WIN_ROWS // 2 and rows % rb == 0 and rows >= NA_WIN_ROWS + 2 * rb
    nblk = rows // rb
    q3, k3, v3 = (a.reshape(batch, seq, width) for a in (q, k, v))
    tab = _na_union_bias(bias_tab, rows)

    def bias_idx(b, i):
        return (jnp.where(i == 0, 0, jnp.where(i == nblk - 1, 2, 1)), 0, 0, 0)

    full = lambda b, i: (b, 0, 0)
    once = pl.Buffered(1)
    out = pl.pallas_call(
        functools.partial(_na_kernel, rows=rows),
        grid=(batch, nblk),
        in_specs=[
            pl.BlockSpec((1, rb * GRID_W, width), lambda b, i: (b, i, 0)),
            pl.BlockSpec((1, seq, width), full, pipeline_mode=once),
            pl.BlockSpec((1, seq, width), full, pipeline_mode=once),
            pl.BlockSpec((1,) + tab.shape[1:], bias_idx, pipeline_mode=once),
        ],
        out_specs=pl.BlockSpec((1, rb * GRID_W, width), lambda b, i: (b, i, 0)),
        out_shape=jax.ShapeDtypeStruct((batch, seq, width), BF16),
        compiler_params=pltpu.CompilerParams(dimension_semantics=("parallel", "arbitrary"),
                                             vmem_limit_bytes=VMEM_LIMIT),
        name="na_attn",
    )(q3, k3, v3, tab)
    return out.reshape(batch * seq, width)


def _mix_kernel(x_ref, yna_ref, p_ref, pprev_ref, pnext_ref, xq_ref, km_ref, vm_ref,
                g_ref, wg_ref, gb_ref, pw_ref, ps_ref, wna_ref, wpool_ref, wxa_ref, wout_ref, fg_ref,
                h_out, hn_out, pbuf, *, seq, tiles_per_seq):
    tm, d = x_ref.shape
    ti = pl.program_id(0) % tiles_per_seq
    x = x_ref[...]
    xb = _rms_rows(x, g_ref[...]).astype(BF16)

    halo = POOL_HALO
    p = p_ref[...]
    pbuf[0:halo, :] = jnp.where(ti == 0, 0.0, pprev_ref[...])
    pbuf[halo:halo + tm, :] = p
    pbuf[halo + tm:2 * halo + tm, :] = jnp.where(ti == tiles_per_seq - 1, 0.0, pnext_ref[...])
    pos = ti * tm + lax.broadcasted_iota(jnp.int32, (tm, 1), 0)
    group = p.shape[-1] // len(POOL_SIZES)
    mixed = []
    for gi, w in enumerate(POOL_SIZES):
        cols = slice(gi * group, (gi + 1) * group)
        tot = pbuf[halo - w // 2:halo - w // 2 + tm, cols]
        for j in range(1 - w // 2, w // 2):
            tot = tot + pbuf[halo + j:halo + j + tm, cols]
        cnt = (jnp.minimum(pos + w // 2, seq) - jnp.maximum(pos - w // 2, 0)).astype(F32)
        pooled = tot / cnt - p[:, cols]
        mixed.append(jnp.dot(pooled.astype(BF16), pw_ref[gi], preferred_element_type=F32))
    y_pool = jnp.concatenate(mixed, axis=-1) * ps_ref[...]

    xq = xq_ref[...]
    xa_hd = xq.shape[-1] // XA_HEADS
    ys = []
    for h in range(XA_HEADS):
        cols = slice(h * xa_hd, (h + 1) * xa_hd)
        s = lax.dot_general(xq[:, cols], km_ref[0, :, cols], _NT, preferred_element_type=F32)
        s = s * (xa_hd ** -0.5)
        e = jnp.exp(s - jnp.max(s, axis=-1, keepdims=True))
        pr = e / jnp.sum(e, axis=-1, keepdims=True)
        ys.append(jnp.dot(pr.astype(BF16), vm_ref[0, :, cols], preferred_element_type=F32))
    y_xa = jnp.concatenate(ys, axis=-1)

    def gate(i):
        z = jnp.dot(xb, wg_ref[:, i * d:(i + 1) * d], preferred_element_type=F32)
        return jax.nn.sigmoid(z + gb_ref[:, i * d:(i + 1) * d])

    merged = (gate(0) * jnp.dot(yna_ref[...], wna_ref[...], preferred_element_type=F32)
              + gate(1) * jnp.dot(y_pool.astype(BF16), wpool_ref[...], preferred_element_type=F32)
              + gate(2) * jnp.dot(y_xa.astype(BF16), wxa_ref[...], preferred_element_type=F32))
    h = x + jnp.dot(merged.astype(BF16), wout_ref[...], preferred_element_type=F32)
    h_out[...] = h
    hn_out[...] = _rms_rows(h, fg_ref[...]).astype(BF16)


def _mix(x2, yna, p, xq, kmem, vmem, g, w_gate, gate_b, pool_w, pool_scale,
         w_na, w_pool, w_xa, w_out, ffn_g, seq):
    n, d = x2.shape
    tm = TOKEN_TILE
    assert seq % tm == 0 and tm % POOL_HALO == 0 and max(POOL_SIZES) // 2 == POOL_HALO
    tiles_per_seq = seq // tm
    halo_blocks = n // POOL_HALO
    per_tile = tm // POOL_HALO
    m = kmem.shape[1]
    na_w, pool_w_, xa_w = yna.shape[1], p.shape[1], xq.shape[1]
    const2 = lambda i: (0, 0)
    const3 = lambda i: (0, 0, 0)
    tile = lambda i: (i, 0)
    mem_idx = lambda i: (i // tiles_per_seq, 0, 0)
    kern = functools.partial(_mix_kernel, seq=seq, tiles_per_seq=tiles_per_seq)
    return pl.pallas_call(
        kern,
        grid=(n // tm,),
        in_specs=[
            pl.BlockSpec((tm, d), tile),
            pl.BlockSpec((tm, na_w), tile),
            pl.BlockSpec((tm, pool_w_), tile),
            pl.BlockSpec((POOL_HALO, pool_w_), lambda i: (jnp.maximum(i * per_tile - 1, 0), 0)),
            pl.BlockSpec((POOL_HALO, pool_w_), lambda i: (jnp.minimum((i + 1) * per_tile, halo_blocks - 1), 0)),
            pl.BlockSpec((tm, xa_w), tile),
            pl.BlockSpec((1, m, xa_w), mem_idx),
            pl.BlockSpec((1, m, xa_w), mem_idx),
            pl.BlockSpec((1, d), const2),
            pl.BlockSpec(w_gate.shape, const2),
            pl.BlockSpec(gate_b.shape, const2),
            pl.BlockSpec(pool_w.shape, const3),
            pl.BlockSpec((1, pool_w_), const2),
            pl.BlockSpec(w_na.shape, const2),
            pl.BlockSpec(w_pool.shape, const2),
            pl.BlockSpec(w_xa.shape, const2),
            pl.BlockSpec(w_out.shape, const2),
            pl.BlockSpec((1, d), const2),
        ],
        out_specs=[pl.BlockSpec((tm, d), tile), pl.BlockSpec((tm, d), tile)],
        out_shape=[jax.ShapeDtypeStruct((n, d), F32), jax.ShapeDtypeStruct((n, d), BF16)],
        scratch_shapes=[pltpu.VMEM((tm + 2 * POOL_HALO, pool_w_), F32)],
        compiler_params=pltpu.CompilerParams(dimension_semantics=("parallel",),
                                             vmem_limit_bytes=VMEM_LIMIT),
        name="mix",
    )(x2, yna, p, p, p, xq, kmem, vmem, g, w_gate, gate_b, pool_w, pool_scale,
      w_na, w_pool, w_xa, w_out, ffn_g)


def _gelu_exact(x):
    return 0.5 * x * (1.0 + lax.erf(x * (2.0 ** -0.5)))


def _candidate_pairs(depth):
    return [(i, j) for i in range(depth) for j in range(depth) if (i + 1) * (j + 1) <= depth]


def _peer_kernel(hn_ref, h_ref, wq_ref, sk_ref, u0_ref, un_ref, vt_ref, o_ref,
                 q_ref, s2_ref, r2_ref, e2_ref, n_ref, e1_ref, top_ref, stat_ref, at0_ref, at1_ref, wa_ref,
                 acc_ref):
    j = pl.program_id(1)
    tt = hn_ref.shape[0]
    keys = PEER_KEYS
    depth = PEER_TOPK + 1
    eb = un_ref.shape[0]
    groups = eb // keys

    def pre_activations(u_blk):
        return lax.dot_general(u_blk, hn_ref[...], _NT, preferred_element_type=F32)

    @pl.when(j == 0)
    def _thresholds():
        lane_tiles = [slice(c * LANES, (c + 1) * LANES) for c in range(tt // LANES)]
        q = jnp.dot(hn_ref[...], wq_ref[...], preferred_element_type=F32).astype(BF16)
        for hp in range(2 * PEER_HEADS):
            q_ref[hp] = q[:, hp * keys:(hp + 1) * keys]
        top_ref[...] = jnp.zeros_like(top_ref)

        def extract(h, carry):
            mine = lax.broadcasted_iota(jnp.int32, (PEER_HEADS, LANES), 0) == h
            for half in range(2):
                s = lax.dot_general(sk_ref[2 * h + half], q_ref[2 * h + half], _NT,
                                    preferred_element_type=F32)
                if half == 0:
                    e1_ref[h] = s
                else:
                    s2_ref[h] = s
                for cols in lane_tiles:
                    x = s[:, cols]
                    rank = jnp.full(x.shape, float(depth), F32)
                    for r in range(depth):
                        m = jnp.max(x, axis=0, keepdims=True)
                        top_ref[half, r, :, cols] = jnp.where(mine, m, top_ref[half, r, :, cols])
                        hit = x >= m
                        if half == 1:
                            rank = jnp.where(hit, float(r), rank)
                        x = jnp.where(hit, -jnp.inf, x)
                    if half == 1:
                        r2_ref[h, :, cols] = rank.astype(BF16)
            return carry

        lax.fori_loop(0, PEER_HEADS, extract, 0)

        for cols in lane_tiles:
            a = [top_ref[0, r, :, cols] for r in range(depth)]
            b = [top_ref[1, r, :, cols] for r in range(depth)]
            cands = [a[i] + b[k] for i, k in _candidate_pairs(depth)]
            best = []
            for r in range(depth):
                m = functools.reduce(jnp.maximum, cands)
                best.append(m)
                if r + 1 < depth:
                    cands = [jnp.where(c >= m, -jnp.inf, c) for c in cands]
            kth, nxt = best[PEER_TOPK - 1], best[PEER_TOPK]
            stat_ref[0, :, cols] = jnp.where(nxt > -jnp.inf, 0.5 * (kth + nxt), kth)
            z = functools.reduce(jnp.add, [jnp.exp(c - best[0]) for c in best[:PEER_TOPK]])
            stat_ref[1, :, cols] = 1.0 / z

        def weights(h, carry):
            mine = lax.broadcasted_iota(jnp.int32, (PEER_HEADS, LANES), 0) == h

            def head_row(tile):
                return jnp.max(jnp.where(mine, tile, -jnp.inf), axis=0, keepdims=True)

            for cols in lane_tiles:
                s1 = e1_ref[h, :, cols]
                theta = head_row(stat_ref[0, :, cols]) - s1
                count = jnp.zeros(s1.shape, F32)
                for r in range(depth):
                    count = jnp.where(head_row(top_ref[1, r, :, cols]) >= theta, float(r + 1), count)
                n_ref[h, :, cols] = count
                e1_ref[h, :, cols] = jnp.exp(s1 - head_row(top_ref[0, 0, :, cols]))
                e2 = (jnp.exp(s2_ref[h, :, cols] - head_row(top_ref[1, 0, :, cols]))
                      * head_row(stat_ref[1, :, cols]))
                e2_ref[h, :, cols] = e2.astype(BF16)
            return carry

        lax.fori_loop(0, PEER_HEADS, weights, 0)
        acc_ref[...] = jnp.zeros_like(acc_ref)
        at0_ref[...] = pre_activations(u0_ref[...])

    rb = 16
    half_rows = eb // 2

    def step(at_cur, at_next):
        at_next[...] = pre_activations(un_ref[...])
        base = pl.multiple_of(j * groups, groups)
        n_tile = [n_ref[h, pl.ds(base, groups), :] for h in range(PEER_HEADS)]
        e1_tile = [e1_ref[h, pl.ds(base, groups), :] for h in range(PEER_HEADS)]
        for l in range(groups):
            row = lambda tile: jnp.broadcast_to(tile[l:l + 1, :], (rb, tt)).astype(BF16)
            cnt = [row(n_tile[h]) for h in range(PEER_HEADS)]
            e1 = [row(e1_tile[h]) for h in range(PEER_HEADS)]
            for blk in range(keys // rb):
                rows = pl.ds(blk * rb, rb)
                dst = pl.ds(l * keys + blk * rb, rb)
                wt = jnp.zeros((rb, tt), BF16)
                for h in range(PEER_HEADS):
                    sel = r2_ref[h, rows, :] < cnt[h]
                    wt = wt + jnp.where(sel, e2_ref[h, rows, :], jnp.zeros((), BF16)) * e1[h]
                wa_ref[dst, :] = wt * _gelu_exact(at_cur[dst, :]).astype(BF16)
            if (l + 1) * keys == half_rows:
                acc_ref[...] += jnp.dot(vt_ref[0, :, :half_rows], wa_ref[:half_rows, :],
                                        preferred_element_type=F32)
        acc_ref[...] += jnp.dot(vt_ref[0, :, half_rows:], wa_ref[half_rows:, :],
                                preferred_element_type=F32)

    @pl.when(j % 2 == 0)
    def _even():
        step(at0_ref, at1_ref)

    @pl.when(j % 2 == 1)
    def _odd():
        step(at1_ref, at0_ref)

    @pl.when(j == pl.num_programs(1) - 1)
    def _finish():
        o_ref[...] = h_ref[...] + acc_ref[...].T


def _peer(hn, h, w_q, sub_keys, u, v):
    n, d = hn.shape
    tt = PEER_TOKEN_TILE
    eb = EXPERT_BLOCK
    experts = u.shape[0]
    assert experts == PEER_KEYS * PEER_KEYS and experts % eb == 0 and eb % PEER_KEYS == 0
    qcols = w_q.shape[1]
    assert qcols == PEER_HEADS * 2 * PEER_KEYS, "the query sub-dimension must equal the lane width"
    depth = PEER_TOPK + 1
    v_t = v.reshape(experts // eb, eb, d).transpose(0, 2, 1)
    return pl.pallas_call(
        _peer_kernel,
        grid=(n // tt, experts // eb),
        in_specs=[
            pl.BlockSpec((tt, d), lambda i, j: (i, 0)),
            pl.BlockSpec((tt, d), lambda i, j: (i, 0)),
            pl.BlockSpec((d, qcols), lambda i, j: (0, 0)),
            pl.BlockSpec(sub_keys.shape, lambda i, j: (0, 0, 0)),
            pl.BlockSpec((eb, d), lambda i, j: (0, 0)),
            pl.BlockSpec((eb, d), lambda i, j: (jnp.minimum(j + 1, experts // eb - 1), 0)),
            pl.BlockSpec((1, d, eb), lambda i, j: (j, 0, 0)),
        ],
        out_specs=pl.BlockSpec((tt, d), lambda i, j: (i, 0)),
        out_shape=jax.ShapeDtypeStruct((n, d), F32),
        scratch_shapes=[
            pltpu.VMEM((2 * PEER_HEADS, tt, PEER_KEYS), BF16),
            pltpu.VMEM((PEER_HEADS, PEER_KEYS, tt), F32),
            pltpu.VMEM((PEER_HEADS, PEER_KEYS, tt), BF16),
            pltpu.VMEM((PEER_HEADS, PEER_KEYS, tt), BF16),
            pltpu.VMEM((PEER_HEADS, PEER_KEYS, tt), F32),
            pltpu.VMEM((PEER_HEADS, PEER_KEYS, tt), F32),
            pltpu.VMEM((2, depth, PEER_HEADS, tt), F32),
            pltpu.VMEM((2, PEER_HEADS, tt), F32),
            pltpu.VMEM((eb, tt), F32),
            pltpu.VMEM((eb, tt), F32),
            pltpu.VMEM((eb, tt), BF16),
            pltpu.VMEM((d, tt), F32),
        ],
        compiler_params=pltpu.CompilerParams(dimension_semantics=("parallel", "arbitrary"),
                                             vmem_limit_bytes=VMEM_LIMIT),
        name="peer",
    )(hn, h, w_q, sub_keys, u, u, v_t)


def kernel(x, mem, mix_norm_g, mem_norm_g, w_in, gate_b, w_mem_kv, na_q_g, na_k_g, na_rpb, pool_w, pool_scale, xa_q_g, xa_k_g, w_branch_na, w_branch_pool, w_branch_xa, w_out, ffn_norm_g, peer_w_q, peer_sub_keys, peer_u, peer_v):
    batch, seq, d = x.shape
    na_width = w_branch_na.shape[1]
    pool_width = w_branch_pool.shape[1]
    xa_width = w_branch_xa.shape[1]
    mix_cols = 3 * na_width + pool_width + xa_width
    n = batch * seq
    assert n % TOKEN_TILE == 0 and n % PEER_TOKEN_TILE == 0 and seq % GRID_W == 0
    row = lambda a: a.reshape(1, -1)
    h = x.reshape(n, d)
    for l in range(mix_norm_g.shape[0]):
        w_in_b = w_in[l].astype(BF16)
        q, k, v, p, xq = _in_proj(
            h, row(mix_norm_g[l]), w_in_b[:, :mix_cols],
            row(jnp.tile(na_q_g[l], NA_HEADS)), row(jnp.tile(na_k_g[l], NA_HEADS)),
            row(jnp.tile(xa_q_g[l], XA_HEADS)), na_width, pool_width, xa_width)
        kmem, vmem = _mem_kv(mem, row(mem_norm_g[l]), w_mem_kv[l].astype(BF16),
                             row(jnp.tile(xa_k_g[l], XA_HEADS)), xa_width)
        y_na = _na_attention(q, k, v, _na_bias_table(na_rpb[l]), batch, seq)
        h, hn = _mix(h, y_na, p, xq, kmem, vmem, row(mix_norm_g[l]), w_in_b[:, mix_cols:],
                     row(gate_b[l]), pool_w[l].astype(BF16), row(pool_scale[l]),
                     w_branch_na[l].astype(BF16), w_branch_pool[l].astype(BF16),
                     w_branch_xa[l].astype(BF16), w_out[l].astype(BF16), row(ffn_norm_g[l]), seq)
        sub_keys = peer_sub_keys[l].reshape(-1, *peer_sub_keys.shape[-2:]).astype(BF16)
        h = _peer(hn, h, peer_w_q[l].astype(BF16), sub_keys,
                  peer_u[l].astype(BF16), peer_v[l].astype(BF16))
    return h.reshape(batch, seq, d)
```

```python
import functools

import numpy as np
import jax
import jax.numpy as jnp
from jax import lax
from jax.experimental import pallas as pl
from jax.experimental.pallas import tpu as pltpu

GRID_W = 64
EPS = 1e-6
NEG_INF = -1e30

NA_HEADS = 8
NA_WIN_ROWS = 8
NA_WIN_COLS = 16
NA_ROWS_PER_STEP = 4
POOL_SIZES = (2, 4, 8, 16)
POOL_HALO = 8
XA_HEADS = 4
PEER_HEADS = 8
PEER_KEYS = 128
PEER_TOPK = 16

LANES = 128
TOKEN_TILE = 256
PEER_TOKEN_TILE = 512
EXPERT_BLOCK = 1024
VMEM_LIMIT = 56 * 1024 * 1024

F32 = jnp.float32
BF16 = jnp.bfloat16
_NT = (((1,), (1,)), ((), ()))


def _rms_rows(x, g):
    return x * lax.rsqrt(jnp.mean(x * x, axis=-1, keepdims=True) + EPS) * g


def _group_mean_sq(v, ones_ref, width):
    sq = (v * v).astype(BF16)
    return jnp.dot(sq, ones_ref[...], preferred_element_type=F32) * (1.0 / width)


def _block_diag_ones(channels, width):
    idx = np.arange(channels) // width
    return jnp.asarray(idx[:, None] == idx[None, :], dtype=BF16)


def _in_proj_kernel(x_ref, g_ref, w_ref, qg_ref, kg_ref, xg_ref, ones_na_ref, ones_xa_ref,
                    q_out, k_out, v_out, p_out, xq_out, *, na_width, pool_width, xa_width,
                    na_head_dim, xa_head_dim):
    xb = _rms_rows(x_ref[...], g_ref[...]).astype(BF16)

    def proj(c0, c1):
        return jnp.dot(xb, w_ref[:, c0:c1], preferred_element_type=F32)

    c = 0
    q = proj(c, c + na_width); c += na_width
    k = proj(c, c + na_width); c += na_width
    v = proj(c, c + na_width); c += na_width
    p = proj(c, c + pool_width); c += pool_width
    xq = proj(c, c + xa_width)
    qn = q * lax.rsqrt(_group_mean_sq(q, ones_na_ref, na_head_dim) + EPS) * qg_ref[...]
    q_out[...] = (qn * (na_head_dim ** -0.5)).astype(BF16)
    kn = k * lax.rsqrt(_group_mean_sq(k, ones_na_ref, na_head_dim) + EPS) * kg_ref[...]
    k_out[...] = kn.astype(BF16)
    v_out[...] = v.astype(BF16)
    p_out[...] = p
    xqn = xq * lax.rsqrt(_group_mean_sq(xq, ones_xa_ref, xa_head_dim) + EPS) * xg_ref[...]
    xq_out[...] = xqn.astype(BF16)


def _in_proj(x2, g, w_cols, qg, kg, xg, na_width, pool_width, xa_width):
    n, d = x2.shape
    tm = TOKEN_TILE
    na_hd = na_width // NA_HEADS
    xa_hd = xa_width // XA_HEADS
    assert na_hd in (4, 16, 64), "the folded attention scale must be a power of two"
    cols = w_cols.shape[1]
    const = lambda i: (0, 0)
    tile = lambda i: (i, 0)
    kern = functools.partial(_in_proj_kernel, na_width=na_width, pool_width=pool_width,
                             xa_width=xa_width, na_head_dim=na_hd, xa_head_dim=xa_hd)
    return pl.pallas_call(
        kern,
        grid=(n // tm,),
        in_specs=[
            pl.BlockSpec((tm, d), tile),
            pl.BlockSpec((1, d), const),
            pl.BlockSpec((d, cols), const),
            pl.BlockSpec((1, na_width), const),
            pl.BlockSpec((1, na_width), const),
            pl.BlockSpec((1, xa_width), const),
            pl.BlockSpec((na_width, na_width), const),
            pl.BlockSpec((xa_width, xa_width), const),
        ],
        out_specs=[
            pl.BlockSpec((tm, na_width), tile),
            pl.BlockSpec((tm, na_width), tile),
            pl.BlockSpec((tm, na_width), tile),
            pl.BlockSpec((tm, pool_width), tile),
            pl.BlockSpec((tm, xa_width), tile),
        ],
        out_shape=[
            jax.ShapeDtypeStruct((n, na_width), BF16),
            jax.ShapeDtypeStruct((n, na_width), BF16),
            jax.ShapeDtypeStruct((n, na_width), BF16),
            jax.ShapeDtypeStruct((n, pool_width), F32),
            jax.ShapeDtypeStruct((n, xa_width), BF16),
        ],
        compiler_params=pltpu.CompilerParams(dimension_semantics=("parallel",),
                                             vmem_limit_bytes=VMEM_LIMIT),
        name="in_proj",
    )(x2, g, w_cols, qg, kg, xg, _block_diag_ones(na_width, na_hd), _block_diag_ones(xa_width, xa_hd))


def _mem_kv_kernel(mem_ref, g_ref, w_ref, kg_ref, ones_ref, k_out, v_out, *, xa_width, xa_head_dim):
    mn = _rms_rows(mem_ref[0], g_ref[...]).astype(BF16)
    kv = jnp.dot(mn, w_ref[...], preferred_element_type=F32)
    k = kv[:, :xa_width]
    kn = k * lax.rsqrt(_group_mean_sq(k, ones_ref, xa_head_dim) + EPS) * kg_ref[...]
    k_out[0] = kn.astype(BF16)
    v_out[0] = kv[:, xa_width:].astype(BF16)


def _mem_kv(mem, g, w_kv, kg, xa_width):
    b, m, d = mem.shape
    xa_hd = xa_width // XA_HEADS
    const = lambda i: (0, 0)
    kern = functools.partial(_mem_kv_kernel, xa_width=xa_width, xa_head_dim=xa_hd)
    return pl.pallas_call(
        kern,
        grid=(b,),
        in_specs=[
            pl.BlockSpec((1, m, d), lambda i: (i, 0, 0)),
            pl.BlockSpec((1, d), const),
            pl.BlockSpec((d, 2 * xa_width), const),
            pl.BlockSpec((1, xa_width), const),
            pl.BlockSpec((xa_width, xa_width), const),
        ],
        out_specs=[pl.BlockSpec((1, m, xa_width), lambda i: (i, 0, 0))] * 2,
        out_shape=[jax.ShapeDtypeStruct((b, m, xa_width), BF16)] * 2,
        compiler_params=pltpu.CompilerParams(dimension_semantics=("parallel",),
                                             vmem_limit_bytes=VMEM_LIMIT),
        name="mem_kv",
    )(mem, g, w_kv, kg, _block_diag_ones(xa_width, xa_hd))


def _na_union_bias(rpb):
    rb, wr, w = NA_ROWS_PER_STEP, NA_WIN_ROWS, GRID_W
    union = wr + rb
    heads = rpb.shape[0]
    qc = np.arange(w)[:, None]
    kc = np.arange(w)[None, :]
    cs = np.clip(qc - NA_WIN_COLS // 2, 0, w - NA_WIN_COLS)
    col_ok = (kc >= cs) & (kc < cs + NA_WIN_COLS)
    pad = w - NA_WIN_COLS
    ext = jnp.pad(rpb.astype(F32), ((0, 0), (0, 0), (pad, pad)), mode="edge")
    by_q = jnp.stack([ext[:, :, w - 1 - c:2 * w - 1 - c] for c in range(w)], axis=2)
    by_q = jnp.where(col_ok[None, None], by_q, NEG_INF)
    patterns = ([(rq, 0) for rq in range(rb)], [(rb, rq) for rq in range(rb)],
                [(rb + rq, rb) for rq in range(rb)])
    pick = np.zeros((len(patterns), rb, union, 2 * wr - 1), np.float32)
    for t, pattern in enumerate(patterns):
        for rq, (d, off) in enumerate(pattern):
            for i in range(wr):
                pick[t, rq, off + i, wr - 1 - d + i] = 1.0
    outside = jnp.asarray((1.0 - pick.sum(-1)) * NEG_INF)
    tab = jnp.einsum("trkd,hdqc->thrqkc", jnp.asarray(pick), by_q, precision=lax.Precision.HIGHEST)
    tab = tab + outside[:, None, :, None, :, None]
    return tab.reshape(len(patterns), heads // 2, 2 * rb * w, union * w)


def _na_kernel(q_ref, k_ref, v_ref, bias_ref, o_ref, *, rows):
    blk = pl.program_id(1)
    rb = NA_ROWS_PER_STEP
    union = (NA_WIN_ROWS + rb) * GRID_W
    ks = jnp.clip(blk * rb - NA_WIN_ROWS // 2, 0, rows - NA_WIN_ROWS - rb)
    start = pl.multiple_of(ks * GRID_W, GRID_W)
    head_dim = q_ref.shape[-1] // NA_HEADS
    tokens = rb * GRID_W
    low = lax.broadcasted_iota(jnp.int32, (tokens, LANES), 1) < head_dim
    outs = []
    for t in range(q_ref.shape[-1] // LANES):
        cols = slice(t * LANES, (t + 1) * LANES)
        qt = q_ref[0, :, cols]
        kt = k_ref[0, pl.ds(start, union), cols]
        vt = v_ref[0, pl.ds(start, union), cols]
        zero = jnp.zeros_like(qt)
        q2 = jnp.concatenate([jnp.where(low, qt, zero), jnp.where(low, zero, qt)], axis=0)
        s = lax.dot_general(q2, kt, _NT, preferred_element_type=F32) + bias_ref[0, t]
        e = jnp.exp(s - jnp.max(s, axis=-1, keepdims=True))
        p = e / jnp.sum(e, axis=-1, keepdims=True)
        o2 = jnp.dot(p.astype(BF16), vt, preferred_element_type=F32)
        outs.append(jnp.where(low, o2[:tokens], o2[tokens:]))
    o_ref[0] = jnp.concatenate(outs, axis=-1).astype(BF16)


def _na_attention(q, k, v, bias_tab, batch, seq):
    width = q.shape[-1]
    rows = seq // GRID_W
    rb = NA_ROWS_PER_STEP
    assert LANES == 2 * (width // NA_HEADS), "two heads per lane tile"
    assert rb == NA_WIN_ROWS // 2 and rows % rb == 0 and rows >= NA_WIN_ROWS + 2 * rb
    nblk = rows // rb
    q3, k3, v3 = (a.reshape(batch, seq, width) for a in (q, k, v))
    tab = bias_tab

    def bias_idx(b, i):
        return (jnp.where(i == 0, 0, jnp.where(i == nblk - 1, 2, 1)), 0, 0, 0)

    full = lambda b, i: (b, 0, 0)
    once = pl.Buffered(1)
    out = pl.pallas_call(
        functools.partial(_na_kernel, rows=rows),
        grid=(batch, nblk),
        in_specs=[
            pl.BlockSpec((1, rb * GRID_W, width), lambda b, i: (b, i, 0)),
            pl.BlockSpec((1, seq, width), full, pipeline_mode=once),
            pl.BlockSpec((1, seq, width), full, pipeline_mode=once),
            pl.BlockSpec((1,) + tab.shape[1:], bias_idx, pipeline_mode=once),
        ],
        out_specs=pl.BlockSpec((1, rb * GRID_W, width), lambda b, i: (b, i, 0)),
        out_shape=jax.ShapeDtypeStruct((batch, seq, width), BF16),
        compiler_params=pltpu.CompilerParams(dimension_semantics=("parallel", "arbitrary"),
                                             vmem_limit_bytes=VMEM_LIMIT),
        name="na_attn",
    )(q3, k3, v3, tab)
    return out.reshape(batch * seq, width)


def _mix_kernel(x_ref, yna_ref, p_ref, pprev_ref, pnext_ref, xq_ref, km_ref, vm_ref,
                g_ref, wg_ref, gb_ref, pw_ref, ps_ref, wna_ref, wpool_ref, wxa_ref, wout_ref, fg_ref,
                h_out, hn_out, pbuf, *, seq, tiles_per_seq):
    tm, d = x_ref.shape
    ti = pl.program_id(0) % tiles_per_seq
    x = x_ref[...]
    xb = _rms_rows(x, g_ref[...]).astype(BF16)

    halo = POOL_HALO
    p = p_ref[...]
    pbuf[0:halo, :] = jnp.where(ti == 0, 0.0, pprev_ref[...])
    pbuf[halo:halo + tm, :] = p
    pbuf[halo + tm:2 * halo + tm, :] = jnp.where(ti == tiles_per_seq - 1, 0.0, pnext_ref[...])
    pos = ti * tm + lax.broadcasted_iota(jnp.int32, (tm, 1), 0)
    group = p.shape[-1] // len(POOL_SIZES)
    mixed = []
    for gi, w in enumerate(POOL_SIZES):
        cols = slice(gi * group, (gi + 1) * group)
        tot = pbuf[halo - w // 2:halo - w // 2 + tm, cols]
        for j in range(1 - w // 2, w // 2):
            tot = tot + pbuf[halo + j:halo + j + tm, cols]
        cnt = (jnp.minimum(pos + w // 2, seq) - jnp.maximum(pos - w // 2, 0)).astype(F32)
        pooled = tot / cnt - p[:, cols]
        mixed.append(jnp.dot(pooled.astype(BF16), pw_ref[gi], preferred_element_type=F32))
    y_pool = jnp.concatenate(mixed, axis=-1) * ps_ref[...]

    xq = xq_ref[...]
    xa_hd = xq.shape[-1] // XA_HEADS
    ys = []
    for h in range(XA_HEADS):
        cols = slice(h * xa_hd, (h + 1) * xa_hd)
        s = lax.dot_general(xq[:, cols], km_ref[0, :, cols], _NT, preferred_element_type=F32)
        s = s * (xa_hd ** -0.5)
        e = jnp.exp(s - jnp.max(s, axis=-1, keepdims=True))
        pr = e / jnp.sum(e, axis=-1, keepdims=True)
        ys.append(jnp.dot(pr.astype(BF16), vm_ref[0, :, cols], preferred_element_type=F32))
    y_xa = jnp.concatenate(ys, axis=-1)

    def gate(i):
        z = jnp.dot(xb, wg_ref[:, i * d:(i + 1) * d], preferred_element_type=F32)
        return jax.nn.sigmoid(z + gb_ref[:, i * d:(i + 1) * d])

    merged = (gate(0) * jnp.dot(yna_ref[...], wna_ref[...], preferred_element_type=F32)
              + gate(1) * jnp.dot(y_pool.astype(BF16), wpool_ref[...], preferred_element_type=F32)
              + gate(2) * jnp.dot(y_xa.astype(BF16), wxa_ref[...], preferred_element_type=F32))
    h = x + jnp.dot(merged.astype(BF16), wout_ref[...], preferred_element_type=F32)
    h_out[...] = h
    hn_out[...] = _rms_rows(h, fg_ref[...]).astype(BF16)


def _mix(x2, yna, p, xq, kmem, vmem, g, w_gate, gate_b, pool_w, pool_scale,
         w_na, w_pool, w_xa, w_out, ffn_g, seq):
    n, d = x2.shape
    tm = TOKEN_TILE
    assert seq % tm == 0 and tm % POOL_HALO == 0 and max(POOL_SIZES) // 2 == POOL_HALO
    tiles_per_seq = seq // tm
    halo_blocks = n // POOL_HALO
    per_tile = tm // POOL_HALO
    m = kmem.shape[1]
    na_w, pool_w_, xa_w = yna.shape[1], p.shape[1], xq.shape[1]
    const2 = lambda i: (0, 0)
    const3 = lambda i: (0, 0, 0)
    tile = lambda i: (i, 0)
    mem_idx = lambda i: (i // tiles_per_seq, 0, 0)
    kern = functools.partial(_mix_kernel, seq=seq, tiles_per_seq=tiles_per_seq)
    return pl.pallas_call(
        kern,
        grid=(n // tm,),
        in_specs=[
            pl.BlockSpec((tm, d), tile),
            pl.BlockSpec((tm, na_w), tile),
            pl.BlockSpec((tm, pool_w_), tile),
            pl.BlockSpec((POOL_HALO, pool_w_), lambda i: (jnp.maximum(i * per_tile - 1, 0), 0)),
            pl.BlockSpec((POOL_HALO, pool_w_), lambda i: (jnp.minimum((i + 1) * per_tile, halo_blocks - 1), 0)),
            pl.BlockSpec((tm, xa_w), tile),
            pl.BlockSpec((1, m, xa_w), mem_idx),
            pl.BlockSpec((1, m, xa_w), mem_idx),
            pl.BlockSpec((1, d), const2),
            pl.BlockSpec(w_gate.shape, const2),
            pl.BlockSpec(gate_b.shape, const2),
            pl.BlockSpec(pool_w.shape, const3),
            pl.BlockSpec((1, pool_w_), const2),
            pl.BlockSpec(w_na.shape, const2),
            pl.BlockSpec(w_pool.shape, const2),
            pl.BlockSpec(w_xa.shape, const2),
            pl.BlockSpec(w_out.shape, const2),
            pl.BlockSpec((1, d), const2),
        ],
        out_specs=[pl.BlockSpec((tm, d), tile), pl.BlockSpec((tm, d), tile)],
        out_shape=[jax.ShapeDtypeStruct((n, d), F32), jax.ShapeDtypeStruct((n, d), BF16)],
        scratch_shapes=[pltpu.VMEM((tm + 2 * POOL_HALO, pool_w_), F32)],
        compiler_params=pltpu.CompilerParams(dimension_semantics=("parallel",),
                                             vmem_limit_bytes=VMEM_LIMIT),
        name="mix",
    )(x2, yna, p, p, p, xq, kmem, vmem, g, w_gate, gate_b, pool_w, pool_scale,
      w_na, w_pool, w_xa, w_out, ffn_g)


def _gelu_exact(x):
    return 0.5 * x * (1.0 + lax.erf(x * (2.0 ** -0.5)))


def _candidate_pairs(depth):
    return [(i, j) for i in range(depth) for j in range(depth) if (i + 1) * (j + 1) <= depth]


def _peer_kernel(hn_ref, h_ref, wq_ref, sk_ref, u0_ref, un_ref, vt_ref, o_ref,
                 s2_ref, r2_ref, e2_ref, n_ref, e1_ref, top_ref, at0_ref, at1_ref, wa_ref, acc_ref):
    j = pl.program_id(1)
    tt = hn_ref.shape[0]
    keys = PEER_KEYS
    depth = PEER_TOPK + 1
    eb = un_ref.shape[0]
    groups = eb // keys

    def pre_activations(u_blk):
        return lax.dot_general(u_blk, hn_ref[...], _NT, preferred_element_type=F32)

    @pl.when(j == 0)
    def _thresholds():
        lane_tiles = [slice(c * LANES, (c + 1) * LANES) for c in range(tt // LANES)]
        q = jnp.dot(hn_ref[...], wq_ref[...], preferred_element_type=F32).astype(BF16)
        for h in range(PEER_HEADS):
            for half in range(2):
                hp = 2 * h + half
                s = lax.dot_general(sk_ref[hp], q[:, hp * keys:(hp + 1) * keys], _NT,
                                    preferred_element_type=F32)
                if half == 0:
                    e1_ref[h] = s
                else:
                    s2_ref[h] = s
                for cols in lane_tiles:
                    x = s[:, cols]
                    rank = jnp.full(x.shape, float(depth), F32)
                    for r in range(depth):
                        m = jnp.max(x, axis=0, keepdims=True)
                        top_ref[half, r, h:h + 1, cols] = m
                        hit = x >= m
                        if half == 1:
                            rank = jnp.where(hit, float(r), rank)
                        x = jnp.where(hit, -jnp.inf, x)
                    if half == 1:
                        r2_ref[h, :, cols] = rank.astype(BF16)
        for cols in lane_tiles:
            a = [top_ref[0, r, :, cols] for r in range(depth)]
            b = [top_ref[1, r, :, cols] for r in range(depth)]
            cands = [a[i] + b[k] for i, k in _candidate_pairs(depth)]
            best = []
            for r in range(depth):
                m = functools.reduce(jnp.maximum, cands)
                best.append(m)
                if r + 1 < depth:
                    cands = [jnp.where(c >= m, -jnp.inf, c) for c in cands]
            kth, nxt = best[PEER_TOPK - 1], best[PEER_TOPK]
            tau = jnp.where(nxt > -jnp.inf, 0.5 * (kth + nxt), kth)
            z = functools.reduce(jnp.add, [jnp.exp(c - best[0]) for c in best[:PEER_TOPK]])
            zinv = 1.0 / z
            for h in range(PEER_HEADS):
                s1 = e1_ref[h, :, cols]
                theta = tau[h:h + 1] - s1
                count = jnp.zeros(s1.shape, F32)
                for r in range(depth):
                    count = jnp.where(b[r][h:h + 1] >= theta, float(r + 1), count)
                n_ref[h, :, cols] = count
                e1_ref[h, :, cols] = jnp.exp(s1 - a[0][h:h + 1])
                e2 = jnp.exp(s2_ref[h, :, cols] - b[0][h:h + 1]) * zinv[h:h + 1]
                e2_ref[h, :, cols] = e2.astype(BF16)
        acc_ref[...] = jnp.zeros_like(acc_ref)
        at0_ref[...] = pre_activations(u0_ref[...])

    rb = 16
    half_rows = eb // 2

    def step(at_cur, at_next):
        at_next[...] = pre_activations(un_ref[...])
        base = pl.multiple_of(j * groups, groups)
        n_tile = [n_ref[h, pl.ds(base, groups), :] for h in range(PEER_HEADS)]
        e1_tile = [e1_ref[h, pl.ds(base, groups), :] for h in range(PEER_HEADS)]
        for l in range(groups):
            row = lambda tile: jnp.broadcast_to(tile[l:l + 1, :], (rb, tt)).astype(BF16)
            cnt = [row(n_tile[h]) for h in range(PEER_HEADS)]
            e1 = [row(e1_tile[h]) for h in range(PEER_HEADS)]
            for blk in range(keys // rb):
                rows = pl.ds(blk * rb, rb)
                dst = pl.ds(l * keys + blk * rb, rb)
                wt = jnp.zeros((rb, tt), BF16)
                for h in range(PEER_HEADS):
                    sel = r2_ref[h, rows, :] < cnt[h]
                    wt = wt + jnp.where(sel, e2_ref[h, rows, :], jnp.zeros((), BF16)) * e1[h]
                wa_ref[dst, :] = wt * _gelu_exact(at_cur[dst, :]).astype(BF16)
            if (l + 1) * keys == half_rows:
                acc_ref[...] += jnp.dot(vt_ref[0, :, :half_rows], wa_ref[:half_rows, :],
                                        preferred_element_type=F32)
        acc_ref[...] += jnp.dot(vt_ref[0, :, half_rows:], wa_ref[half_rows:, :],
                                preferred_element_type=F32)

    @pl.when(j % 2 == 0)
    def _even():
        step(at0_ref, at1_ref)

    @pl.when(j % 2 == 1)
    def _odd():
        step(at1_ref, at0_ref)

    @pl.when(j == pl.num_programs(1) - 1)
    def _finish():
        o_ref[...] = h_ref[...] + acc_ref[...].T


def _peer(hn, h, w_q, sub_keys, u, v):
    n, d = hn.shape
    tt = PEER_TOKEN_TILE
    eb = EXPERT_BLOCK
    experts = u.shape[0]
    assert experts == PEER_KEYS * PEER_KEYS and experts % eb == 0 and eb % PEER_KEYS == 0
    qcols = w_q.shape[1]
    assert qcols == PEER_HEADS * 2 * PEER_KEYS, "the query sub-dimension must equal the lane width"
    depth = PEER_TOPK + 1
    v_t = v.reshape(experts // eb, eb, d).transpose(0, 2, 1)
    return pl.pallas_call(
        _peer_kernel,
        grid=(n // tt, experts // eb),
        in_specs=[
            pl.BlockSpec((tt, d), lambda i, j: (i, 0)),
            pl.BlockSpec((tt, d), lambda i, j: (i, 0)),
            pl.BlockSpec((d, qcols), lambda i, j: (0, 0)),
            pl.BlockSpec(sub_keys.shape, lambda i, j: (0, 0, 0)),
            pl.BlockSpec((eb, d), lambda i, j: (0, 0)),
            pl.BlockSpec((eb, d), lambda i, j: (jnp.minimum(j + 1, experts // eb - 1), 0)),
            pl.BlockSpec((1, d, eb), lambda i, j: (j, 0, 0)),
        ],
        out_specs=pl.BlockSpec((tt, d), lambda i, j: (i, 0)),
        out_shape=jax.ShapeDtypeStruct((n, d), F32),
        scratch_shapes=[
            pltpu.VMEM((PEER_HEADS, PEER_KEYS, tt), F32),
            pltpu.VMEM((PEER_HEADS, PEER_KEYS, tt), BF16),
            pltpu.VMEM((PEER_HEADS, PEER_KEYS, tt), BF16),
            pltpu.VMEM((PEER_HEADS, PEER_KEYS, tt), F32),
            pltpu.VMEM((PEER_HEADS, PEER_KEYS, tt), F32),
            pltpu.VMEM((2, depth, PEER_HEADS, tt), F32),
            pltpu.VMEM((eb, tt), F32),
            pltpu.VMEM((eb, tt), F32),
            pltpu.VMEM((eb, tt), BF16),
            pltpu.VMEM((d, tt), F32),
        ],
        compiler_params=pltpu.CompilerParams(dimension_semantics=("parallel", "arbitrary"),
                                             vmem_limit_bytes=VMEM_LIMIT),
        name="peer",
    )(hn, h, w_q, sub_keys, u, u, v_t)


def kernel(x, mem, mix_norm_g, mem_norm_g, w_in, gate_b, w_mem_kv, na_q_g, na_k_g, na_rpb, pool_w, pool_scale, xa_q_g, xa_k_g, w_branch_na, w_branch_pool, w_branch_xa, w_out, ffn_norm_g, peer_w_q, peer_sub_keys, peer_u, peer_v):
    batch, seq, d = x.shape
    na_width = w_branch_na.shape[1]
    pool_width = w_branch_pool.shape[1]
    xa_width = w_branch_xa.shape[1]
    mix_cols = 3 * na_width + pool_width + xa_width
    n = batch * seq
    assert n % TOKEN_TILE == 0 and n % PEER_TOKEN_TILE == 0 and seq % GRID_W == 0
    row = lambda a: a.reshape(1, -1)
    h = x.reshape(n, d)
    for l in range(mix_norm_g.shape[0]):
        w_in_b = w_in[l].astype(BF16)
        q, k, v, p, xq = _in_proj(
            h, row(mix_norm_g[l]), w_in_b[:, :mix_cols],
            row(jnp.tile(na_q_g[l], NA_HEADS)), row(jnp.tile(na_k_g[l], NA_HEADS)),
            row(jnp.tile(xa_q_g[l], XA_HEADS)), na_width, pool_width, xa_width)
        kmem, vmem = _mem_kv(mem, row(mem_norm_g[l]), w_mem_kv[l].astype(BF16),
                             row(jnp.tile(xa_k_g[l], XA_HEADS)), xa_width)
        y_na = _na_attention(q, k, v, _na_union_bias(na_rpb[l]), batch, seq)
        h, hn = _mix(h, y_na, p, xq, kmem, vmem, row(mix_norm_g[l]), w_in_b[:, mix_cols:],
                     row(gate_b[l]), pool_w[l].astype(BF16), row(pool_scale[l]),
                     w_branch_na[l].astype(BF16), w_branch_pool[l].astype(BF16),
                     w_branch_xa[l].astype(BF16), w_out[l].astype(BF16), row(ffn_norm_g[l]), seq)
        sub_keys = peer_sub_keys[l].reshape(-1, *peer_sub_keys.shape[-2:]).astype(BF16)
        h = _peer(hn, h, peer_w_q[l].astype(BF16), sub_keys,
                  peer_u[l].astype(BF16), peer_v[l].astype(BF16))
    return h.reshape(batch, seq, d)
```

```python
import functools

import numpy as np
import jax
import jax.numpy as jnp
from jax import lax
from jax.experimental import pallas as pl
from jax.experimental.pallas import tpu as pltpu

GRID_W = 64
EPS = 1e-6
NEG_INF = -1e30

NA_HEADS = 8
NA_WIN_ROWS = 8
NA_WIN_COLS = 16
NA_ROWS_PER_STEP = 4
POOL_SIZES = (2, 4, 8, 16)
POOL_HALO = 8
XA_HEADS = 4
PEER_HEADS = 8
PEER_KEYS = 128
PEER_TOPK = 16

LANES = 128
TOKEN_TILE = 256
PEER_TOKEN_TILE = 512
EXPERT_BLOCK = 1024
VMEM_LIMIT = 56 * 1024 * 1024

F32 = jnp.float32
BF16 = jnp.bfloat16
_NT = (((1,), (1,)), ((), ()))


def _rms_rows(x, g):
    return x * lax.rsqrt(jnp.mean(x * x, axis=-1, keepdims=True) + EPS) * g


def _group_mean_sq(v, ones_ref, width):
    sq = (v * v).astype(BF16)
    return jnp.dot(sq, ones_ref[...], preferred_element_type=F32) * (1.0 / width)


def _block_diag_ones(channels, width):
    idx = np.arange(channels) // width
    return jnp.asarray(idx[:, None] == idx[None, :], dtype=BF16)


def _in_proj_kernel(x_ref, g_ref, w_ref, qg_ref, kg_ref, xg_ref, ones_na_ref, ones_xa_ref,
                    q_out, k_out, v_out, p_out, xq_out, *, na_width, pool_width, xa_width,
                    na_head_dim, xa_head_dim):
    xb = _rms_rows(x_ref[...], g_ref[...]).astype(BF16)

    def proj(c0, c1):
        return jnp.dot(xb, w_ref[:, c0:c1], preferred_element_type=F32)

    c = 0
    q = proj(c, c + na_width); c += na_width
    k = proj(c, c + na_width); c += na_width
    v = proj(c, c + na_width); c += na_width
    p = proj(c, c + pool_width); c += pool_width
    xq = proj(c, c + xa_width)
    qn = q * lax.rsqrt(_group_mean_sq(q, ones_na_ref, na_head_dim) + EPS) * qg_ref[...]
    q_out[...] = (qn * (na_head_dim ** -0.5)).astype(BF16)
    kn = k * lax.rsqrt(_group_mean_sq(k, ones_na_ref, na_head_dim) + EPS) * kg_ref[...]
    k_out[...] = kn.astype(BF16)
    v_out[...] = v.astype(BF16)
    p_out[...] = p
    xqn = xq * lax.rsqrt(_group_mean_sq(xq, ones_xa_ref, xa_head_dim) + EPS) * xg_ref[...]
    xq_out[...] = xqn.astype(BF16)


def _in_proj(x2, g, w_cols, qg, kg, xg, na_width, pool_width, xa_width):
    n, d = x2.shape
    tm = TOKEN_TILE
    na_hd = na_width // NA_HEADS
    xa_hd = xa_width // XA_HEADS
    assert na_hd in (4, 16, 64), "the folded attention scale must be a power of two"
    cols = w_cols.shape[1]
    const = lambda i: (0, 0)
    tile = lambda i: (i, 0)
    kern = functools.partial(_in_proj_kernel, na_width=na_width, pool_width=pool_width,
                             xa_width=xa_width, na_head_dim=na_hd, xa_head_dim=xa_hd)
    return pl.pallas_call(
        kern,
        grid=(n // tm,),
        in_specs=[
            pl.BlockSpec((tm, d), tile),
            pl.BlockSpec((1, d), const),
            pl.BlockSpec((d, cols), const),
            pl.BlockSpec((1, na_width), const),
            pl.BlockSpec((1, na_width), const),
            pl.BlockSpec((1, xa_width), const),
            pl.BlockSpec((na_width, na_width), const),
            pl.BlockSpec((xa_width, xa_width), const),
        ],
        out_specs=[
            pl.BlockSpec((tm, na_width), tile),
            pl.BlockSpec((tm, na_width), tile),
            pl.BlockSpec((tm, na_width), tile),
            pl.BlockSpec((tm, pool_width), tile),
            pl.BlockSpec((tm, xa_width), tile),
        ],
        out_shape=[
            jax.ShapeDtypeStruct((n, na_width), BF16),
            jax.ShapeDtypeStruct((n, na_width), BF16),
            jax.ShapeDtypeStruct((n, na_width), BF16),
            jax.ShapeDtypeStruct((n, pool_width), F32),
            jax.ShapeDtypeStruct((n, xa_width), BF16),
        ],
        compiler_params=pltpu.CompilerParams(dimension_semantics=("parallel",),
                                             vmem_limit_bytes=VMEM_LIMIT),
        name="in_proj",
    )(x2, g, w_cols, qg, kg, xg, _block_diag_ones(na_width, na_hd), _block_diag_ones(xa_width, xa_hd))


def _mem_kv_kernel(mem_ref, g_ref, w_ref, kg_ref, ones_ref, k_out, v_out, *, xa_width, xa_head_dim):
    mn = _rms_rows(mem_ref[0], g_ref[...]).astype(BF16)
    kv = jnp.dot(mn, w_ref[...], preferred_element_type=F32)
    k = kv[:, :xa_width]
    kn = k * lax.rsqrt(_group_mean_sq(k, ones_ref, xa_head_dim) + EPS) * kg_ref[...]
    k_out[0] = kn.astype(BF16)
    v_out[0] = kv[:, xa_width:].astype(BF16)


def _mem_kv(mem, g, w_kv, kg, xa_width):
    b, m, d = mem.shape
    xa_hd = xa_width // XA_HEADS
    const = lambda i: (0, 0)
    kern = functools.partial(_mem_kv_kernel, xa_width=xa_width, xa_head_dim=xa_hd)
    return pl.pallas_call(
        kern,
        grid=(b,),
        in_specs=[
            pl.BlockSpec((1, m, d), lambda i: (i, 0, 0)),
            pl.BlockSpec((1, d), const),
            pl.BlockSpec((d, 2 * xa_width), const),
            pl.BlockSpec((1, xa_width), const),
            pl.BlockSpec((xa_width, xa_width), const),
        ],
        out_specs=[pl.BlockSpec((1, m, xa_width), lambda i: (i, 0, 0))] * 2,
        out_shape=[jax.ShapeDtypeStruct((b, m, xa_width), BF16)] * 2,
        compiler_params=pltpu.CompilerParams(dimension_semantics=("parallel",),
                                             vmem_limit_bytes=VMEM_LIMIT),
        name="mem_kv",
    )(mem, g, w_kv, kg, _block_diag_ones(xa_width, xa_hd))


def _na_union_bias(rpb):
    rb, wr, w = NA_ROWS_PER_STEP, NA_WIN_ROWS, GRID_W
    union = wr + rb
    heads = rpb.shape[0]
    qc = np.arange(w)[:, None]
    kc = np.arange(w)[None, :]
    cs = np.clip(qc - NA_WIN_COLS // 2, 0, w - NA_WIN_COLS)
    col_ok = (kc >= cs) & (kc < cs + NA_WIN_COLS)
    pad = w - NA_WIN_COLS
    ext = jnp.pad(rpb.astype(F32), ((0, 0), (0, 0), (pad, pad)), mode="edge")
    by_q = jnp.stack([ext[:, :, w - 1 - c:2 * w - 1 - c] for c in range(w)], axis=2)
    by_q = jnp.where(col_ok[None, None], by_q, NEG_INF)
    patterns = ([(rq, 0) for rq in range(rb)], [(rb, rq) for rq in range(rb)],
                [(rb + rq, rb) for rq in range(rb)])
    pick = np.zeros((len(patterns), rb, union, 2 * wr), np.float32)
    pick[..., 2 * wr - 1] = 1.0
    for t, pattern in enumerate(patterns):
        for rq, (d, off) in enumerate(pattern):
            for i in range(wr):
                pick[t, rq, off + i, wr - 1 - d + i] = 1.0
                pick[t, rq, off + i, 2 * wr - 1] = 0.0
    by_q = jnp.concatenate([by_q, jnp.full((heads, 1, w, w), NEG_INF, F32)], axis=1)
    tab = jnp.einsum("trkd,hdqc->thrqkc", jnp.asarray(pick), by_q, precision=lax.Precision.HIGHEST)
    return tab.reshape(len(patterns), heads // 2, 2 * rb * w, union * w)


def _na_kernel(q_ref, k_ref, v_ref, bias_ref, o_ref, *, rows):
    blk = pl.program_id(1)
    rb = NA_ROWS_PER_STEP
    union = (NA_WIN_ROWS + rb) * GRID_W
    ks = jnp.clip(blk * rb - NA_WIN_ROWS // 2, 0, rows - NA_WIN_ROWS - rb)
    start = pl.multiple_of(ks * GRID_W, GRID_W)
    head_dim = q_ref.shape[-1] // NA_HEADS
    tokens = rb * GRID_W
    low = lax.broadcasted_iota(jnp.int32, (tokens, LANES), 1) < head_dim
    outs = []
    for t in range(q_ref.shape[-1] // LANES):
        cols = slice(t * LANES, (t + 1) * LANES)
        qt = q_ref[0, :, cols]
        kt = k_ref[0, pl.ds(start, union), cols]
        vt = v_ref[0, pl.ds(start, union), cols]
        zero = jnp.zeros_like(qt)
        q2 = jnp.concatenate([jnp.where(low, qt, zero), jnp.where(low, zero, qt)], axis=0)
        s = lax.dot_general(q2, kt, _NT, preferred_element_type=F32) + bias_ref[0, t]
        e = jnp.exp(s - jnp.max(s, axis=-1, keepdims=True))
        p = e / jnp.sum(e, axis=-1, keepdims=True)
        o2 = jnp.dot(p.astype(BF16), vt, preferred_element_type=F32)
        outs.append(jnp.where(low, o2[:tokens], o2[tokens:]))
    o_ref[0] = jnp.concatenate(outs, axis=-1).astype(BF16)


def _na_attention(q, k, v, bias_tab, batch, seq):
    width = q.shape[-1]
    rows = seq // GRID_W
    rb = NA_ROWS_PER_STEP
    assert LANES == 2 * (width // NA_HEADS), "two heads per lane tile"
    assert rb == NA_WIN_ROWS // 2 and rows % rb == 0 and rows >= NA_WIN_ROWS + 2 * rb
    nblk = rows // rb
    q3, k3, v3 = (a.reshape(batch, seq, width) for a in (q, k, v))
    tab = bias_tab

    def bias_idx(b, i):
        return (jnp.where(i == 0, 0, jnp.where(i == nblk - 1, 2, 1)), 0, 0, 0)

    full = lambda b, i: (b, 0, 0)
    once = pl.Buffered(1)
    out = pl.pallas_call(
        functools.partial(_na_kernel, rows=rows),
        grid=(batch, nblk),
        in_specs=[
            pl.BlockSpec((1, rb * GRID_W, width), lambda b, i: (b, i, 0)),
            pl.BlockSpec((1, seq, width), full, pipeline_mode=once),
            pl.BlockSpec((1, seq, width), full, pipeline_mode=once),
            pl.BlockSpec((1,) + tab.shape[1:], bias_idx, pipeline_mode=once),
        ],
        out_specs=pl.BlockSpec((1, rb * GRID_W, width), lambda b, i: (b, i, 0)),
        out_shape=jax.ShapeDtypeStruct((batch, seq, width), BF16),
        compiler_params=pltpu.CompilerParams(dimension_semantics=("parallel", "arbitrary"),
                                             vmem_limit_bytes=VMEM_LIMIT),
        name="na_attn",
    )(q3, k3, v3, tab)
    return out.reshape(batch * seq, width)


def _mix_kernel(x_ref, yna_ref, p_ref, pprev_ref, pnext_ref, xq_ref, km_ref, vm_ref,
                g_ref, wg_ref, gb_ref, pw_ref, ps_ref, wna_ref, wpool_ref, wxa_ref, wout_ref, fg_ref,
                h_out, hn_out, pbuf, *, seq, tiles_per_seq):
    tm, d = x_ref.shape
    ti = pl.program_id(0) % tiles_per_seq
    x = x_ref[...]
    xb = _rms_rows(x, g_ref[...]).astype(BF16)

    halo = POOL_HALO
    p = p_ref[...]
    pbuf[0:halo, :] = jnp.where(ti == 0, 0.0, pprev_ref[...])
    pbuf[halo:halo + tm, :] = p
    pbuf[halo + tm:2 * halo + tm, :] = jnp.where(ti == tiles_per_seq - 1, 0.0, pnext_ref[...])
    pos = ti * tm + lax.broadcasted_iota(jnp.int32, (tm, 1), 0)
    group = p.shape[-1] // len(POOL_SIZES)
    mixed = []
    for gi, w in enumerate(POOL_SIZES):
        cols = slice(gi * group, (gi + 1) * group)
        tot = pbuf[halo - w // 2:halo - w // 2 + tm, cols]
        for j in range(1 - w // 2, w // 2):
            tot = tot + pbuf[halo + j:halo + j + tm, cols]
        cnt = (jnp.minimum(pos + w // 2, seq) - jnp.maximum(pos - w // 2, 0)).astype(F32)
        pooled = tot / cnt - p[:, cols]
        mixed.append(jnp.dot(pooled.astype(BF16), pw_ref[gi], preferred_element_type=F32))
    y_pool = jnp.concatenate(mixed, axis=-1) * ps_ref[...]

    xq = xq_ref[...]
    xa_hd = xq.shape[-1] // XA_HEADS
    ys = []
    for h in range(XA_HEADS):
        cols = slice(h * xa_hd, (h + 1) * xa_hd)
        s = lax.dot_general(xq[:, cols], km_ref[0, :, cols], _NT, preferred_element_type=F32)
        s = s * (xa_hd ** -0.5)
        e = jnp.exp(s - jnp.max(s, axis=-1, keepdims=True))
        pr = e / jnp.sum(e, axis=-1, keepdims=True)
        ys.append(jnp.dot(pr.astype(BF16), vm_ref[0, :, cols], preferred_element_type=F32))
    y_xa = jnp.concatenate(ys, axis=-1)

    def gate(i):
        z = jnp.dot(xb, wg_ref[:, i * d:(i + 1) * d], preferred_element_type=F32)
        return jax.nn.sigmoid(z + gb_ref[:, i * d:(i + 1) * d])

    merged = (gate(0) * jnp.dot(yna_ref[...], wna_ref[...], preferred_element_type=F32)
              + gate(1) * jnp.dot(y_pool.astype(BF16), wpool_ref[...], preferred_element_type=F32)
              + gate(2) * jnp.dot(y_xa.astype(BF16), wxa_ref[...], preferred_element_type=F32))
    h = x + jnp.dot(merged.astype(BF16), wout_ref[...], preferred_element_type=F32)
    h_out[...] = h
    hn_out[...] = _rms_rows(h, fg_ref[...]).astype(BF16)


def _mix(x2, yna, p, xq, kmem, vmem, g, w_gate, gate_b, pool_w, pool_scale,
         w_na, w_pool, w_xa, w_out, ffn_g, seq):
    n, d = x2.shape
    tm = TOKEN_TILE
    assert seq % tm == 0 and tm % POOL_HALO == 0 and max(POOL_SIZES) // 2 == POOL_HALO
    tiles_per_seq = seq // tm
    halo_blocks = n // POOL_HALO
    per_tile = tm // POOL_HALO
    m = kmem.shape[1]
    na_w, pool_w_, xa_w = yna.shape[1], p.shape[1], xq.shape[1]
    const2 = lambda i: (0, 0)
    const3 = lambda i: (0, 0, 0)
    tile = lambda i: (i, 0)
    mem_idx = lambda i: (i // tiles_per_seq, 0, 0)
    kern = functools.partial(_mix_kernel, seq=seq, tiles_per_seq=tiles_per_seq)
    return pl.pallas_call(
        kern,
        grid=(n // tm,),
        in_specs=[
            pl.BlockSpec((tm, d), tile),
            pl.BlockSpec((tm, na_w), tile),
            pl.BlockSpec((tm, pool_w_), tile),
            pl.BlockSpec((POOL_HALO, pool_w_), lambda i: (jnp.maximum(i * per_tile - 1, 0), 0)),
            pl.BlockSpec((POOL_HALO, pool_w_), lambda i: (jnp.minimum((i + 1) * per_tile, halo_blocks - 1), 0)),
            pl.BlockSpec((tm, xa_w), tile),
            pl.BlockSpec((1, m, xa_w), mem_idx),
            pl.BlockSpec((1, m, xa_w), mem_idx),
            pl.BlockSpec((1, d), const2),
            pl.BlockSpec(w_gate.shape, const2),
            pl.BlockSpec(gate_b.shape, const2),
            pl.BlockSpec(pool_w.shape, const3),
            pl.BlockSpec((1, pool_w_), const2),
            pl.BlockSpec(w_na.shape, const2),
            pl.BlockSpec(w_pool.shape, const2),
            pl.BlockSpec(w_xa.shape, const2),
            pl.BlockSpec(w_out.shape, const2),
            pl.BlockSpec((1, d), const2),
        ],
        out_specs=[pl.BlockSpec((tm, d), tile), pl.BlockSpec((tm, d), tile)],
        out_shape=[jax.ShapeDtypeStruct((n, d), F32), jax.ShapeDtypeStruct((n, d), BF16)],
        scratch_shapes=[pltpu.VMEM((tm + 2 * POOL_HALO, pool_w_), F32)],
        compiler_params=pltpu.CompilerParams(dimension_semantics=("parallel",),
                                             vmem_limit_bytes=VMEM_LIMIT),
        name="mix",
    )(x2, yna, p, p, p, xq, kmem, vmem, g, w_gate, gate_b, pool_w, pool_scale,
      w_na, w_pool, w_xa, w_out, ffn_g)


def _gelu_exact(x):
    return 0.5 * x * (1.0 + lax.erf(x * (2.0 ** -0.5)))


def _candidate_pairs(depth):
    return [(i, j) for i in range(depth) for j in range(depth) if (i + 1) * (j + 1) <= depth]


def _descending_maxima(x, count):
    sub = 8
    tiles = [x[i * sub:(i + 1) * sub, :] for i in range(x.shape[0] // sub)]
    assert len(tiles) % 4 == 0
    stacks = []
    for g in range(0, len(tiles), 4):
        t = tiles[g:g + 4]
        for i, j in ((0, 1), (2, 3), (0, 2), (1, 3), (1, 2)):
            t[i], t[j] = jnp.maximum(t[i], t[j]), jnp.minimum(t[i], t[j])
        stacks.append(t)
    out = []
    for _ in range(count):
        m = jnp.max(functools.reduce(jnp.maximum, [t[0] for t in stacks]), axis=0, keepdims=True)
        out.append(m)
        for t in stacks:
            hit = t[0] >= m
            for lvl in range(3):
                t[lvl] = jnp.where(hit, t[lvl + 1], t[lvl])
            t[3] = jnp.where(hit, -jnp.inf, t[3])
    return out


def _peer_kernel(hn_ref, h_ref, wq_ref, sk_ref, u0_ref, un_ref, vt_ref, o_ref,
                 s1_ref, s2_ref, r2_ref, e2_ref, n_ref, e1_ref, top_ref, at0_ref, at1_ref, wa_ref, acc_ref):
    j = pl.program_id(1)
    tt = hn_ref.shape[0]
    keys = PEER_KEYS
    depth = PEER_TOPK + 1
    eb = un_ref.shape[0]
    groups = eb // keys

    def pre_activations(u_blk):
        return lax.dot_general(u_blk, hn_ref[...], _NT, preferred_element_type=F32)

    @pl.when(j == 0)
    def _thresholds():
        lane_tiles = [slice(c * LANES, (c + 1) * LANES) for c in range(tt // LANES)]
        q = jnp.dot(hn_ref[...], wq_ref[...], preferred_element_type=F32).astype(BF16)
        for h in range(PEER_HEADS):
            for half in range(2):
                hp = 2 * h + half
                s = lax.dot_general(sk_ref[hp], q[:, hp * keys:(hp + 1) * keys], _NT,
                                    preferred_element_type=F32)
                if half == 0:
                    s1_ref[h] = s
                else:
                    s2_ref[h] = s
                for cols in lane_tiles:
                    x = s[:, cols]
                    if half == 0:
                        for r, m in enumerate(_descending_maxima(x, depth)):
                            top_ref[half, r, h:h + 1, cols] = m
                        continue
                    rank = jnp.full(x.shape, float(depth), F32)
                    for r in range(depth):
                        m = jnp.max(x, axis=0, keepdims=True)
                        top_ref[half, r, h:h + 1, cols] = m
                        hit = x >= m
                        rank = jnp.where(hit, float(r), rank)
                        x = jnp.where(hit, -jnp.inf, x)
                    r2_ref[h, :, cols] = rank.astype(BF16)
        for c, cols in enumerate(lane_tiles):
            a = [top_ref[0, r, :, cols] for r in range(depth)]
            b = [top_ref[1, r, :, cols] for r in range(depth)]
            cands = [a[i] + b[k] for i, k in _candidate_pairs(depth)]
            best = []
            for r in range(depth):
                m = functools.reduce(jnp.maximum, cands)
                best.append(m)
                if r + 1 < depth:
                    cands = [jnp.where(c >= m, -jnp.inf, c) for c in cands]
            kth, nxt = best[PEER_TOPK - 1], best[PEER_TOPK]
            tau = jnp.where(nxt > -jnp.inf, 0.5 * (kth + nxt), kth)
            z = functools.reduce(jnp.add, [jnp.exp(c - best[0]) for c in best[:PEER_TOPK]])
            zinv = 1.0 / z
            for h in range(PEER_HEADS):
                s1 = s1_ref[h, :, cols]
                theta = tau[h:h + 1] - s1
                count = jnp.zeros(s1.shape, F32)
                for r in range(depth):
                    count = jnp.where(b[r][h:h + 1] >= theta, float(r + 1), count)
                n_ref[h, c] = count
                e1_ref[h, c] = jnp.exp(s1 - a[0][h:h + 1])
                e2 = jnp.exp(s2_ref[h, :, cols] - b[0][h:h + 1]) * zinv[h:h + 1]
                e2_ref[h, :, cols] = e2.astype(BF16)
        acc_ref[...] = jnp.zeros_like(acc_ref)
        at0_ref[...] = pre_activations(u0_ref[...])

    rb = 16
    half_rows = eb // 2

    def step(at_cur, at_next):
        at_next[...] = pre_activations(un_ref[...])
        for l in range(groups):
            i1 = j * groups + l

            def row(ref, h, c):
                return jnp.broadcast_to(ref[h, c, pl.ds(i1, 1), :], (rb, LANES)).astype(BF16)

            for c in range(tt // LANES):
                cols = slice(c * LANES, (c + 1) * LANES)
                cnt = [row(n_ref, h, c) for h in range(PEER_HEADS)]
                e1 = [row(e1_ref, h, c) for h in range(PEER_HEADS)]
                for blk in range(keys // rb):
                    rows = pl.ds(blk * rb, rb)
                    dst = pl.ds(l * keys + blk * rb, rb)
                    wt = None
                    for h in range(PEER_HEADS):
                        sel = r2_ref[h, rows, cols] < cnt[h]
                        w = jnp.where(sel, e2_ref[h, rows, cols], jnp.zeros((), BF16)) * e1[h]
                        wt = w if wt is None else wt + w
                    wa_ref[dst, cols] = wt * _gelu_exact(at_cur[dst, cols]).astype(BF16)
            if (l + 1) * keys == half_rows:
                acc_ref[...] += jnp.dot(vt_ref[0, :, :half_rows], wa_ref[:half_rows, :],
                                        preferred_element_type=F32)
        acc_ref[...] += jnp.dot(vt_ref[0, :, half_rows:], wa_ref[half_rows:, :],
                                preferred_element_type=F32)

    @pl.when(j % 2 == 0)
    def _even():
        step(at0_ref, at1_ref)

    @pl.when(j % 2 == 1)
    def _odd():
        step(at1_ref, at0_ref)

    @pl.when(j == pl.num_programs(1) - 1)
    def _finish():
        o_ref[...] = h_ref[...] + acc_ref[...].T


def _peer(hn, h, w_q, sub_keys, u, v):
    n, d = hn.shape
    tt = PEER_TOKEN_TILE
    eb = EXPERT_BLOCK
    experts = u.shape[0]
    assert experts == PEER_KEYS * PEER_KEYS and experts % eb == 0 and eb % PEER_KEYS == 0
    qcols = w_q.shape[1]
    assert qcols == PEER_HEADS * 2 * PEER_KEYS, "the query sub-dimension must equal the lane width"
    depth = PEER_TOPK + 1
    v_t = v.reshape(experts // eb, eb, d).transpose(0, 2, 1)
    return pl.pallas_call(
        _peer_kernel,
        grid=(n // tt, experts // eb),
        in_specs=[
            pl.BlockSpec((tt, d), lambda i, j: (i, 0)),
            pl.BlockSpec((tt, d), lambda i, j: (i, 0)),
            pl.BlockSpec((d, qcols), lambda i, j: (0, 0)),
            pl.BlockSpec(sub_keys.shape, lambda i, j: (0, 0, 0)),
            pl.BlockSpec((eb, d), lambda i, j: (0, 0)),
            pl.BlockSpec((eb, d), lambda i, j: (jnp.minimum(j + 1, experts // eb - 1), 0)),
            pl.BlockSpec((1, d, eb), lambda i, j: (j, 0, 0)),
        ],
        out_specs=pl.BlockSpec((tt, d), lambda i, j: (i, 0)),
        out_shape=jax.ShapeDtypeStruct((n, d), F32),
        scratch_shapes=[
            pltpu.VMEM((PEER_HEADS, PEER_KEYS, tt), F32),
            pltpu.VMEM((PEER_HEADS, PEER_KEYS, tt), F32),
            pltpu.VMEM((PEER_HEADS, PEER_KEYS, tt), BF16),
            pltpu.VMEM((PEER_HEADS, PEER_KEYS, tt), BF16),
            pltpu.VMEM((PEER_HEADS, tt // LANES, PEER_KEYS, LANES), F32),
            pltpu.VMEM((PEER_HEADS, tt // LANES, PEER_KEYS, LANES), F32),
            pltpu.VMEM((2, depth, PEER_HEADS, tt), F32),
            pltpu.VMEM((eb, tt), F32),
            pltpu.VMEM((eb, tt), F32),
            pltpu.VMEM((eb, tt), BF16),
            pltpu.VMEM((d, tt), F32),
        ],
        compiler_params=pltpu.CompilerParams(dimension_semantics=("parallel", "arbitrary"),
                                             vmem_limit_bytes=VMEM_LIMIT),
        name="peer",
    )(hn, h, w_q, sub_keys, u, u, v_t)


def kernel(x, mem, mix_norm_g, mem_norm_g, w_in, gate_b, w_mem_kv, na_q_g, na_k_g, na_rpb, pool_w, pool_scale, xa_q_g, xa_k_g, w_branch_na, w_branch_pool, w_branch_xa, w_out, ffn_norm_g, peer_w_q, peer_sub_keys, peer_u, peer_v):
    batch, seq, d = x.shape
    na_width = w_branch_na.shape[1]
    pool_width = w_branch_pool.shape[1]
    xa_width = w_branch_xa.shape[1]
    mix_cols = 3 * na_width + pool_width + xa_width
    n = batch * seq
    assert n % TOKEN_TILE == 0 and n % PEER_TOKEN_TILE == 0 and seq % GRID_W == 0
    row = lambda a: a.reshape(1, -1)
    h = x.reshape(n, d)
    for l in range(mix_norm_g.shape[0]):
        w_in_b = w_in[l].astype(BF16)
        q, k, v, p, xq = _in_proj(
            h, row(mix_norm_g[l]), w_in_b[:, :mix_cols],
            row(jnp.tile(na_q_g[l], NA_HEADS)), row(jnp.tile(na_k_g[l], NA_HEADS)),
            row(jnp.tile(xa_q_g[l], XA_HEADS)), na_width, pool_width, xa_width)
        kmem, vmem = _mem_kv(mem, row(mem_norm_g[l]), w_mem_kv[l].astype(BF16),
                             row(jnp.tile(xa_k_g[l], XA_HEADS)), xa_width)
        y_na = _na_attention(q, k, v, _na_union_bias(na_rpb[l]), batch, seq)
        h, hn = _mix(h, y_na, p, xq, kmem, vmem, row(mix_norm_g[l]), w_in_b[:, mix_cols:],
                     row(gate_b[l]), pool_w[l].astype(BF16), row(pool_scale[l]),
                     w_branch_na[l].astype(BF16), w_branch_pool[l].astype(BF16),
                     w_branch_xa[l].astype(BF16), w_out[l].astype(BF16), row(ffn_norm_g[l]), seq)
        sub_keys = peer_sub_keys[l].reshape(-1, *peer_sub_keys.shape[-2:]).astype(BF16)
        h = _peer(hn, h, peer_w_q[l].astype(BF16), sub_keys,
                  peer_u[l].astype(BF16), peer_v[l].astype(BF16))
    return h.reshape(batch, seq, d)
```

```python
import functools

import numpy as np
import jax
import jax.numpy as jnp
from jax import lax
from jax.experimental import pallas as pl
from jax.experimental.pallas import tpu as pltpu

GRID_W = 64
EPS = 1e-6
NEG_INF = -1e30

NA_HEADS = 8
NA_WIN_ROWS = 8
NA_WIN_COLS = 16
NA_ROWS_PER_STEP = 4
POOL_SIZES = (2, 4, 8, 16)
POOL_HALO = 8
XA_HEADS = 4
PEER_HEADS = 8
PEER_KEYS = 128
PEER_TOPK = 16

LANES = 128
TOKEN_TILE = 256
PEER_TOKEN_TILE = 512
EXPERT_BLOCK = 1024
VMEM_LIMIT = 56 * 1024 * 1024

F32 = jnp.float32
BF16 = jnp.bfloat16
_NT = (((1,), (1,)), ((), ()))


def _rms_rows(x, g):
    return x * lax.rsqrt(jnp.mean(x * x, axis=-1, keepdims=True) + EPS) * g


def _group_mean_sq(v, ones_ref, width):
    sq = (v * v).astype(BF16)
    return jnp.dot(sq, ones_ref[...], preferred_element_type=F32) * (1.0 / width)


def _block_diag_ones(channels, width):
    idx = np.arange(channels) // width
    return jnp.asarray(idx[:, None] == idx[None, :], dtype=BF16)


def _in_proj_kernel(x_ref, g_ref, w_ref, qg_ref, kg_ref, xg_ref, ones_na_ref, ones_xa_ref,
                    q_out, k_out, v_out, p_out, xq_out, *, na_width, pool_width, xa_width,
                    na_head_dim, xa_head_dim):
    xb = _rms_rows(x_ref[...], g_ref[...]).astype(BF16)

    def proj(c0, c1):
        return jnp.dot(xb, w_ref[:, c0:c1], preferred_element_type=F32)

    c = 0
    q = proj(c, c + na_width); c += na_width
    k = proj(c, c + na_width); c += na_width
    v = proj(c, c + na_width); c += na_width
    p = proj(c, c + pool_width); c += pool_width
    xq = proj(c, c + xa_width)
    qn = q * lax.rsqrt(_group_mean_sq(q, ones_na_ref, na_head_dim) + EPS) * qg_ref[...]
    q_out[...] = (qn * (na_head_dim ** -0.5)).astype(BF16)
    kn = k * lax.rsqrt(_group_mean_sq(k, ones_na_ref, na_head_dim) + EPS) * kg_ref[...]
    k_out[...] = kn.astype(BF16)
    v_out[...] = v.astype(BF16)
    p_out[...] = p
    xqn = xq * lax.rsqrt(_group_mean_sq(xq, ones_xa_ref, xa_head_dim) + EPS) * xg_ref[...]
    xq_out[...] = xqn.astype(BF16)


def _in_proj(x2, g, w_cols, qg, kg, xg, na_width, pool_width, xa_width):
    n, d = x2.shape
    tm = TOKEN_TILE
    na_hd = na_width // NA_HEADS
    xa_hd = xa_width // XA_HEADS
    assert na_hd in (4, 16, 64), "the folded attention scale must be a power of two"
    cols = w_cols.shape[1]
    const = lambda i: (0, 0)
    tile = lambda i: (i, 0)
    kern = functools.partial(_in_proj_kernel, na_width=na_width, pool_width=pool_width,
                             xa_width=xa_width, na_head_dim=na_hd, xa_head_dim=xa_hd)
    return pl.pallas_call(
        kern,
        grid=(n // tm,),
        in_specs=[
            pl.BlockSpec((tm, d), tile),
            pl.BlockSpec((1, d), const),
            pl.BlockSpec((d, cols), const),
            pl.BlockSpec((1, na_width), const),
            pl.BlockSpec((1, na_width), const),
            pl.BlockSpec((1, xa_width), const),
            pl.BlockSpec((na_width, na_width), const),
            pl.BlockSpec((xa_width, xa_width), const),
        ],
        out_specs=[
            pl.BlockSpec((tm, na_width), tile),
            pl.BlockSpec((tm, na_width), tile),
            pl.BlockSpec((tm, na_width), tile),
            pl.BlockSpec((tm, pool_width), tile),
            pl.BlockSpec((tm, xa_width), tile),
        ],
        out_shape=[
            jax.ShapeDtypeStruct((n, na_width), BF16),
            jax.ShapeDtypeStruct((n, na_width), BF16),
            jax.ShapeDtypeStruct((n, na_width), BF16),
            jax.ShapeDtypeStruct((n, pool_width), F32),
            jax.ShapeDtypeStruct((n, xa_width), BF16),
        ],
        compiler_params=pltpu.CompilerParams(dimension_semantics=("parallel",),
                                             vmem_limit_bytes=VMEM_LIMIT),
        name="in_proj",
    )(x2, g, w_cols, qg, kg, xg, _block_diag_ones(na_width, na_hd), _block_diag_ones(xa_width, xa_hd))


def _mem_kv_kernel(mem_ref, g_ref, w_ref, kg_ref, ones_ref, k_out, v_out, *, xa_width, xa_head_dim):
    mn = _rms_rows(mem_ref[0], g_ref[...]).astype(BF16)
    kv = jnp.dot(mn, w_ref[...], preferred_element_type=F32)
    k = kv[:, :xa_width]
    kn = k * lax.rsqrt(_group_mean_sq(k, ones_ref, xa_head_dim) + EPS) * kg_ref[...]
    k_out[0] = kn.astype(BF16)
    v_out[0] = kv[:, xa_width:].astype(BF16)


def _mem_kv(mem, g, w_kv, kg, xa_width):
    b, m, d = mem.shape
    xa_hd = xa_width // XA_HEADS
    const = lambda i: (0, 0)
    kern = functools.partial(_mem_kv_kernel, xa_width=xa_width, xa_head_dim=xa_hd)
    return pl.pallas_call(
        kern,
        grid=(b,),
        in_specs=[
            pl.BlockSpec((1, m, d), lambda i: (i, 0, 0)),
            pl.BlockSpec((1, d), const),
            pl.BlockSpec((d, 2 * xa_width), const),
            pl.BlockSpec((1, xa_width), const),
            pl.BlockSpec((xa_width, xa_width), const),
        ],
        out_specs=[pl.BlockSpec((1, m, xa_width), lambda i: (i, 0, 0))] * 2,
        out_shape=[jax.ShapeDtypeStruct((b, m, xa_width), BF16)] * 2,
        compiler_params=pltpu.CompilerParams(dimension_semantics=("parallel",),
                                             vmem_limit_bytes=VMEM_LIMIT),
        name="mem_kv",
    )(mem, g, w_kv, kg, _block_diag_ones(xa_width, xa_hd))


def _na_union_bias(rpb):
    rb, wr, w = NA_ROWS_PER_STEP, NA_WIN_ROWS, GRID_W
    union = wr + rb
    heads = rpb.shape[0]
    qc = np.arange(w)[:, None]
    kc = np.arange(w)[None, :]
    cs = np.clip(qc - NA_WIN_COLS // 2, 0, w - NA_WIN_COLS)
    col_ok = (kc >= cs) & (kc < cs + NA_WIN_COLS)
    pad = w - NA_WIN_COLS
    ext = jnp.pad(rpb.astype(F32), ((0, 0), (0, 0), (pad, pad)), mode="edge")
    by_q = jnp.stack([ext[:, :, w - 1 - c:2 * w - 1 - c] for c in range(w)], axis=2)
    by_q = jnp.where(col_ok[None, None], by_q, NEG_INF)
    patterns = ([(rq, 0) for rq in range(rb)], [(rb, rq) for rq in range(rb)],
                [(rb + rq, rb) for rq in range(rb)])
    pick = np.zeros((len(patterns), rb, union, 2 * wr), np.float32)
    pick[..., 2 * wr - 1] = 1.0
    for t, pattern in enumerate(patterns):
        for rq, (d, off) in enumerate(pattern):
            for i in range(wr):
                pick[t, rq, off + i, wr - 1 - d + i] = 1.0
                pick[t, rq, off + i, 2 * wr - 1] = 0.0
    by_q = jnp.concatenate([by_q, jnp.full((heads, 1, w, w), NEG_INF, F32)], axis=1)
    tab = jnp.einsum("trkd,hdqc->thrqkc", jnp.asarray(pick), by_q, precision=lax.Precision.HIGHEST)
    return tab.reshape(len(patterns), heads // 2, 2 * rb * w, union * w)


def _na_kernel(q_ref, k_ref, v_ref, bias_ref, o_ref, *, rows):
    blk = pl.program_id(1)
    rb = NA_ROWS_PER_STEP
    union = (NA_WIN_ROWS + rb) * GRID_W
    ks = jnp.clip(blk * rb - NA_WIN_ROWS // 2, 0, rows - NA_WIN_ROWS - rb)
    start = pl.multiple_of(ks * GRID_W, GRID_W)
    head_dim = q_ref.shape[-1] // NA_HEADS
    tokens = rb * GRID_W
    low = lax.broadcasted_iota(jnp.int32, (tokens, LANES), 1) < head_dim
    outs = []
    for t in range(q_ref.shape[-1] // LANES):
        cols = slice(t * LANES, (t + 1) * LANES)
        qt = q_ref[0, :, cols]
        kt = k_ref[0, pl.ds(start, union), cols]
        vt = v_ref[0, pl.ds(start, union), cols]
        zero = jnp.zeros_like(qt)
        q2 = jnp.concatenate([jnp.where(low, qt, zero), jnp.where(low, zero, qt)], axis=0)
        s = lax.dot_general(q2, kt, _NT, preferred_element_type=F32) + bias_ref[0, t]
        e = jnp.exp(s - jnp.max(s, axis=-1, keepdims=True))
        p = e / jnp.sum(e, axis=-1, keepdims=True)
        o2 = jnp.dot(p.astype(BF16), vt, preferred_element_type=F32)
        outs.append(jnp.where(low, o2[:tokens], o2[tokens:]))
    o_ref[0] = jnp.concatenate(outs, axis=-1).astype(BF16)


def _na_attention(q, k, v, bias_tab, batch, seq):
    width = q.shape[-1]
    rows = seq // GRID_W
    rb = NA_ROWS_PER_STEP
    assert LANES == 2 * (width // NA_HEADS), "two heads per lane tile"
    assert rb == NA_WIN_ROWS // 2 and rows % rb == 0 and rows >= NA_WIN_ROWS + 2 * rb
    nblk = rows // rb
    q3, k3, v3 = (a.reshape(batch, seq, width) for a in (q, k, v))
    tab = bias_tab

    def bias_idx(b, i):
        return (jnp.where(i == 0, 0, jnp.where(i == nblk - 1, 2, 1)), 0, 0, 0)

    full = lambda b, i: (b, 0, 0)
    once = pl.Buffered(1)
    out = pl.pallas_call(
        functools.partial(_na_kernel, rows=rows),
        grid=(batch, nblk),
        in_specs=[
            pl.BlockSpec((1, rb * GRID_W, width), lambda b, i: (b, i, 0)),
            pl.BlockSpec((1, seq, width), full, pipeline_mode=once),
            pl.BlockSpec((1, seq, width), full, pipeline_mode=once),
            pl.BlockSpec((1,) + tab.shape[1:], bias_idx, pipeline_mode=once),
        ],
        out_specs=pl.BlockSpec((1, rb * GRID_W, width), lambda b, i: (b, i, 0)),
        out_shape=jax.ShapeDtypeStruct((batch, seq, width), BF16),
        compiler_params=pltpu.CompilerParams(dimension_semantics=("parallel", "arbitrary"),
                                             vmem_limit_bytes=VMEM_LIMIT),
        name="na_attn",
    )(q3, k3, v3, tab)
    return out.reshape(batch * seq, width)


def _mix_kernel(x_ref, yna_ref, p_ref, pprev_ref, pnext_ref, xq_ref, km_ref, vm_ref,
                g_ref, wg_ref, gb_ref, pw_ref, ps_ref, wna_ref, wpool_ref, wxa_ref, wout_ref, fg_ref,
                h_out, hn_out, pbuf, *, seq, tiles_per_seq):
    tm, d = x_ref.shape
    ti = pl.program_id(0) % tiles_per_seq
    x = x_ref[...]
    xb = _rms_rows(x, g_ref[...]).astype(BF16)

    halo = POOL_HALO
    p = p_ref[...]
    pbuf[0:halo, :] = jnp.where(ti == 0, 0.0, pprev_ref[...])
    pbuf[halo:halo + tm, :] = p
    pbuf[halo + tm:2 * halo + tm, :] = jnp.where(ti == tiles_per_seq - 1, 0.0, pnext_ref[...])
    pos = ti * tm + lax.broadcasted_iota(jnp.int32, (tm, 1), 0)
    group = p.shape[-1] // len(POOL_SIZES)
    mixed = []
    for gi, w in enumerate(POOL_SIZES):
        cols = slice(gi * group, (gi + 1) * group)
        tot = pbuf[halo - w // 2:halo - w // 2 + tm, cols]
        for j in range(1 - w // 2, w // 2):
            tot = tot + pbuf[halo + j:halo + j + tm, cols]
        cnt = (jnp.minimum(pos + w // 2, seq) - jnp.maximum(pos - w // 2, 0)).astype(F32)
        pooled = tot / cnt - p[:, cols]
        mixed.append(jnp.dot(pooled.astype(BF16), pw_ref[gi], preferred_element_type=F32))
    y_pool = jnp.concatenate(mixed, axis=-1) * ps_ref[...]

    xq = xq_ref[...]
    xa_hd = xq.shape[-1] // XA_HEADS
    ys = []
    for h in range(XA_HEADS):
        cols = slice(h * xa_hd, (h + 1) * xa_hd)
        s = lax.dot_general(xq[:, cols], km_ref[0, :, cols], _NT, preferred_element_type=F32)
        s = s * (xa_hd ** -0.5)
        e = jnp.exp(s - jnp.max(s, axis=-1, keepdims=True))
        pr = e / jnp.sum(e, axis=-1, keepdims=True)
        ys.append(jnp.dot(pr.astype(BF16), vm_ref[0, :, cols], preferred_element_type=F32))
    y_xa = jnp.concatenate(ys, axis=-1)

    def gate(i):
        z = jnp.dot(xb, wg_ref[:, i * d:(i + 1) * d], preferred_element_type=F32)
        return jax.nn.sigmoid(z + gb_ref[:, i * d:(i + 1) * d])

    merged = (gate(0) * jnp.dot(yna_ref[...], wna_ref[...], preferred_element_type=F32)
              + gate(1) * jnp.dot(y_pool.astype(BF16), wpool_ref[...], preferred_element_type=F32)
              + gate(2) * jnp.dot(y_xa.astype(BF16), wxa_ref[...], preferred_element_type=F32))
    h = x + jnp.dot(merged.astype(BF16), wout_ref[...], preferred_element_type=F32)
    h_out[...] = h
    hn_out[...] = _rms_rows(h, fg_ref[...]).astype(BF16)


def _mix(x2, yna, p, xq, kmem, vmem, g, w_gate, gate_b, pool_w, pool_scale,
         w_na, w_pool, w_xa, w_out, ffn_g, seq):
    n, d = x2.shape
    tm = TOKEN_TILE
    assert seq % tm == 0 and tm % POOL_HALO == 0 and max(POOL_SIZES) // 2 == POOL_HALO
    tiles_per_seq = seq // tm
    halo_blocks = n // POOL_HALO
    per_tile = tm // POOL_HALO
    m = kmem.shape[1]
    na_w, pool_w_, xa_w = yna.shape[1], p.shape[1], xq.shape[1]
    const2 = lambda i: (0, 0)
    const3 = lambda i: (0, 0, 0)
    tile = lambda i: (i, 0)
    mem_idx = lambda i: (i // tiles_per_seq, 0, 0)
    kern = functools.partial(_mix_kernel, seq=seq, tiles_per_seq=tiles_per_seq)
    return pl.pallas_call(
        kern,
        grid=(n // tm,),
        in_specs=[
            pl.BlockSpec((tm, d), tile),
            pl.BlockSpec((tm, na_w), tile),
            pl.BlockSpec((tm, pool_w_), tile),
            pl.BlockSpec((POOL_HALO, pool_w_), lambda i: (jnp.maximum(i * per_tile - 1, 0), 0)),
            pl.BlockSpec((POOL_HALO, pool_w_), lambda i: (jnp.minimum((i + 1) * per_tile, halo_blocks - 1), 0)),
            pl.BlockSpec((tm, xa_w), tile),
            pl.BlockSpec((1, m, xa_w), mem_idx),
            pl.BlockSpec((1, m, xa_w), mem_idx),
            pl.BlockSpec((1, d), const2),
            pl.BlockSpec(w_gate.shape, const2),
            pl.BlockSpec(gate_b.shape, const2),
            pl.BlockSpec(pool_w.shape, const3),
            pl.BlockSpec((1, pool_w_), const2),
            pl.BlockSpec(w_na.shape, const2),
            pl.BlockSpec(w_pool.shape, const2),
            pl.BlockSpec(w_xa.shape, const2),
            pl.BlockSpec(w_out.shape, const2),
            pl.BlockSpec((1, d), const2),
        ],
        out_specs=[pl.BlockSpec((tm, d), tile), pl.BlockSpec((tm, d), tile)],
        out_shape=[jax.ShapeDtypeStruct((n, d), F32), jax.ShapeDtypeStruct((n, d), BF16)],
        scratch_shapes=[pltpu.VMEM((tm + 2 * POOL_HALO, pool_w_), F32)],
        compiler_params=pltpu.CompilerParams(dimension_semantics=("parallel",),
                                             vmem_limit_bytes=VMEM_LIMIT),
        name="mix",
    )(x2, yna, p, p, p, xq, kmem, vmem, g, w_gate, gate_b, pool_w, pool_scale,
      w_na, w_pool, w_xa, w_out, ffn_g)


def _gelu_exact(x):
    return 0.5 * x * (1.0 + lax.erf(x * (2.0 ** -0.5)))


def _candidate_pairs(depth):
    return [(i, j) for i in range(depth) for j in range(depth) if (i + 1) * (j + 1) <= depth]


def _descending_maxima(x, count):
    sub = 8
    tiles = [x[i * sub:(i + 1) * sub, :] for i in range(x.shape[0] // sub)]
    assert len(tiles) % 4 == 0
    stacks = []
    for g in range(0, len(tiles), 4):
        t = tiles[g:g + 4]
        for i, j in ((0, 1), (2, 3), (0, 2), (1, 3), (1, 2)):
            t[i], t[j] = jnp.maximum(t[i], t[j]), jnp.minimum(t[i], t[j])
        stacks.append(t)
    out = []
    for _ in range(count):
        m = jnp.max(functools.reduce(jnp.maximum, [t[0] for t in stacks]), axis=0, keepdims=True)
        out.append(m)
        for t in stacks:
            hit = t[0] >= m
            for lvl in range(3):
                t[lvl] = jnp.where(hit, t[lvl + 1], t[lvl])
            t[3] = jnp.where(hit, -jnp.inf, t[3])
    return out


def _peer_kernel(hn_ref, h_ref, wq_ref, sk_ref, u0_ref, un_ref, vt_ref, o_ref,
                 s2_ref, r2_ref, e2_ref, n_ref, e1_ref, top_ref, at0_ref, at1_ref, wa_ref, acc_ref):
    j = pl.program_id(1)
    tt = hn_ref.shape[0]
    keys = PEER_KEYS
    depth = PEER_TOPK + 1
    eb = un_ref.shape[0]
    groups = eb // keys

    def pre_activations(u_blk):
        return lax.dot_general(u_blk, hn_ref[...], _NT, preferred_element_type=F32)

    @pl.when(j == 0)
    def _thresholds():
        lane_tiles = [slice(c * LANES, (c + 1) * LANES) for c in range(tt // LANES)]
        q = jnp.dot(hn_ref[...], wq_ref[...], preferred_element_type=F32).astype(BF16)
        for h in range(PEER_HEADS):
            for half in range(2):
                hp = 2 * h + half
                s = lax.dot_general(sk_ref[hp], q[:, hp * keys:(hp + 1) * keys], _NT,
                                    preferred_element_type=F32)
                if half == 0:
                    e1_ref[h] = s
                else:
                    s2_ref[h] = s
                for cols in lane_tiles:
                    x = s[:, cols]
                    if half == 0:
                        for r, m in enumerate(_descending_maxima(x, depth)):
                            top_ref[half, r, h:h + 1, cols] = m
                        continue
                    rank = jnp.full(x.shape, float(depth), F32)
                    for r in range(depth):
                        m = jnp.max(x, axis=0, keepdims=True)
                        top_ref[half, r, h:h + 1, cols] = m
                        hit = x >= m
                        rank = jnp.where(hit, float(r), rank)
                        x = jnp.where(hit, -jnp.inf, x)
                    r2_ref[h, :, cols] = rank.astype(BF16)
        for cols in lane_tiles:
            a = [top_ref[0, r, :, cols] for r in range(depth)]
            b = [top_ref[1, r, :, cols] for r in range(depth)]
            cands = [a[i] + b[k] for i, k in _candidate_pairs(depth)]
            best = []
            for r in range(depth):
                m = functools.reduce(jnp.maximum, cands)
                best.append(m)
                if r + 1 < depth:
                    cands = [jnp.where(c >= m, -jnp.inf, c) for c in cands]
            kth, nxt = best[PEER_TOPK - 1], best[PEER_TOPK]
            tau = jnp.where(nxt > -jnp.inf, 0.5 * (kth + nxt), kth)
            z = functools.reduce(jnp.add, [jnp.exp(c - best[0]) for c in best[:PEER_TOPK]])
            zinv = 1.0 / z
            for h in range(PEER_HEADS):
                s1 = e1_ref[h, :, cols]
                theta = tau[h:h + 1] - s1
                count = jnp.zeros(s1.shape, F32)
                for r in range(depth):
                    count = jnp.where(b[r][h:h + 1] >= theta, float(r + 1), count)
                n_ref[h, :, cols] = count
                e1_ref[h, :, cols] = jnp.exp(s1 - a[0][h:h + 1])
                e2 = jnp.exp(s2_ref[h, :, cols] - b[0][h:h + 1]) * zinv[h:h + 1]
                e2_ref[h, :, cols] = e2.astype(BF16)
        acc_ref[...] = jnp.zeros_like(acc_ref)
        at0_ref[...] = pre_activations(u0_ref[...])

    rb = 16
    half_rows = eb // 2

    def step(at_cur, at_next):
        at_next[...] = pre_activations(un_ref[...])
        base = pl.multiple_of(j * groups, groups)
        n_tile = [n_ref[h, pl.ds(base, groups), :] for h in range(PEER_HEADS)]
        e1_tile = [e1_ref[h, pl.ds(base, groups), :] for h in range(PEER_HEADS)]
        for l in range(groups):
            row = lambda tile: jnp.broadcast_to(tile[l:l + 1, :], (rb, tt)).astype(BF16)
            cnt = [row(n_tile[h]) for h in range(PEER_HEADS)]
            e1 = [row(e1_tile[h]) for h in range(PEER_HEADS)]
            for blk in range(keys // rb):
                rows = pl.ds(blk * rb, rb)
                dst = pl.ds(l * keys + blk * rb, rb)
                wt = None
                for h in range(PEER_HEADS):
                    sel = r2_ref[h, rows, :] < cnt[h]
                    w = jnp.where(sel, e2_ref[h, rows, :], jnp.zeros((), BF16)) * e1[h]
                    wt = w if wt is None else wt + w
                wa_ref[dst, :] = wt * _gelu_exact(at_cur[dst, :].astype(BF16))
            if (l + 1) * keys == half_rows:
                acc_ref[...] += jnp.dot(vt_ref[0, :, :half_rows], wa_ref[:half_rows, :],
                                        preferred_element_type=F32)
        acc_ref[...] += jnp.dot(vt_ref[0, :, half_rows:], wa_ref[half_rows:, :],
                                preferred_element_type=F32)

    @pl.when(j % 2 == 0)
    def _even():
        step(at0_ref, at1_ref)

    @pl.when(j % 2 == 1)
    def _odd():
        step(at1_ref, at0_ref)

    @pl.when(j == pl.num_programs(1) - 1)
    def _finish():
        o_ref[...] = h_ref[...] + acc_ref[...].T


def _peer(hn, h, w_q, sub_keys, u, v):
    n, d = hn.shape
    tt = PEER_TOKEN_TILE
    eb = EXPERT_BLOCK
    experts = u.shape[0]
    assert experts == PEER_KEYS * PEER_KEYS and experts % eb == 0 and eb % PEER_KEYS == 0
    qcols = w_q.shape[1]
    assert qcols == PEER_HEADS * 2 * PEER_KEYS, "the query sub-dimension must equal the lane width"
    depth = PEER_TOPK + 1
    v_t = v.reshape(experts // eb, eb, d).transpose(0, 2, 1)
    return pl.pallas_call(
        _peer_kernel,
        grid=(n // tt, experts // eb),
        in_specs=[
            pl.BlockSpec((tt, d), lambda i, j: (i, 0)),
            pl.BlockSpec((tt, d), lambda i, j: (i, 0)),
            pl.BlockSpec((d, qcols), lambda i, j: (0, 0)),
            pl.BlockSpec(sub_keys.shape, lambda i, j: (0, 0, 0)),
            pl.BlockSpec((eb, d), lambda i, j: (0, 0)),
            pl.BlockSpec((eb, d), lambda i, j: (jnp.minimum(j + 1, experts // eb - 1), 0)),
            pl.BlockSpec((1, d, eb), lambda i, j: (j, 0, 0)),
        ],
        out_specs=pl.BlockSpec((tt, d), lambda i, j: (i, 0)),
        out_shape=jax.ShapeDtypeStruct((n, d), F32),
        scratch_shapes=[
            pltpu.VMEM((PEER_HEADS, PEER_KEYS, tt), F32),
            pltpu.VMEM((PEER_HEADS, PEER_KEYS, tt), BF16),
            pltpu.VMEM((PEER_HEADS, PEER_KEYS, tt), BF16),
            pltpu.VMEM((PEER_HEADS, PEER_KEYS, tt), F32),
            pltpu.VMEM((PEER_HEADS, PEER_KEYS, tt), F32),
            pltpu.VMEM((2, depth, PEER_HEADS, tt), F32),
            pltpu.VMEM((eb, tt), F32),
            pltpu.VMEM((eb, tt), F32),
            pltpu.VMEM((eb, tt), BF16),
            pltpu.VMEM((d, tt), F32),
        ],
        compiler_params=pltpu.CompilerParams(dimension_semantics=("parallel", "arbitrary"),
                                             vmem_limit_bytes=VMEM_LIMIT),
        name="peer",
    )(hn, h, w_q, sub_keys, u, u, v_t)


def kernel(x, mem, mix_norm_g, mem_norm_g, w_in, gate_b, w_mem_kv, na_q_g, na_k_g, na_rpb, pool_w, pool_scale, xa_q_g, xa_k_g, w_branch_na, w_branch_pool, w_branch_xa, w_out, ffn_norm_g, peer_w_q, peer_sub_keys, peer_u, peer_v):
    batch, seq, d = x.shape
    na_width = w_branch_na.shape[1]
    pool_width = w_branch_pool.shape[1]
    xa_width = w_branch_xa.shape[1]
    mix_cols = 3 * na_width + pool_width + xa_width
    n = batch * seq
    assert n % TOKEN_TILE == 0 and n % PEER_TOKEN_TILE == 0 and seq % GRID_W == 0
    row = lambda a: a.reshape(1, -1)
    h = x.reshape(n, d)
    for l in range(mix_norm_g.shape[0]):
        w_in_b = w_in[l].astype(BF16)
        q, k, v, p, xq = _in_proj(
            h, row(mix_norm_g[l]), w_in_b[:, :mix_cols],
            row(jnp.tile(na_q_g[l], NA_HEADS)), row(jnp.tile(na_k_g[l], NA_HEADS)),
            row(jnp.tile(xa_q_g[l], XA_HEADS)), na_width, pool_width, xa_width)
        kmem, vmem = _mem_kv(mem, row(mem_norm_g[l]), w_mem_kv[l].astype(BF16),
                             row(jnp.tile(xa_k_g[l], XA_HEADS)), xa_width)
        y_na = _na_attention(q, k, v, _na_union_bias(na_rpb[l]), batch, seq)
        h, hn = _mix(h, y_na, p, xq, kmem, vmem, row(mix_norm_g[l]), w_in_b[:, mix_cols:],
                     row(gate_b[l]), pool_w[l].astype(BF16), row(pool_scale[l]),
                     w_branch_na[l].astype(BF16), w_branch_pool[l].astype(BF16),
                     w_branch_xa[l].astype(BF16), w_out[l].astype(BF16), row(ffn_norm_g[l]), seq)
        sub_keys = peer_sub_keys[l].reshape(-1, *peer_sub_keys.shape[-2:]).astype(BF16)
        h = _peer(hn, h, peer_w_q[l].astype(BF16), sub_keys,
                  peer_u[l].astype(BF16), peer_v[l].astype(BF16))
    return h.reshape(batch, seq, d)
```

```python
import functools

import numpy as np
import jax
import jax.numpy as jnp
from jax import lax
from jax.experimental import pallas as pl
from jax.experimental.pallas import tpu as pltpu

GRID_W = 64
EPS = 1e-6
NEG_INF = -1e30

NA_HEADS = 8
NA_WIN_ROWS = 8
NA_WIN_COLS = 16
NA_ROWS_PER_STEP = 4
POOL_SIZES = (2, 4, 8, 16)
POOL_HALO = 8
XA_HEADS = 4
PEER_HEADS = 8
PEER_KEYS = 128
PEER_TOPK = 16

LANES = 128
TOKEN_TILE = 256
PEER_TOKEN_TILE = 512
EXPERT_BLOCK = 1024
VMEM_LIMIT = 56 * 1024 * 1024

F32 = jnp.float32
BF16 = jnp.bfloat16
_NT = (((1,), (1,)), ((), ()))


def _rms_rows(x, g):
    return x * lax.rsqrt(jnp.mean(x * x, axis=-1, keepdims=True) + EPS) * g


def _group_mean_sq(v, ones_ref, width):
    sq = (v * v).astype(BF16)
    return jnp.dot(sq, ones_ref[...], preferred_element_type=F32) * (1.0 / width)


def _block_diag_ones(channels, width):
    idx = np.arange(channels) // width
    return jnp.asarray(idx[:, None] == idx[None, :], dtype=BF16)


def _in_proj_kernel(x_ref, g_ref, w_ref, qg_ref, kg_ref, xg_ref, ones_na_ref, ones_xa_ref,
                    q_out, k_out, v_out, p_out, xq_out, *, na_width, pool_width, xa_width,
                    na_head_dim, xa_head_dim):
    xb = _rms_rows(x_ref[...], g_ref[...]).astype(BF16)

    def proj(c0, c1):
        return jnp.dot(xb, w_ref[:, c0:c1], preferred_element_type=F32)

    c = 0
    q = proj(c, c + na_width); c += na_width
    k = proj(c, c + na_width); c += na_width
    v = proj(c, c + na_width); c += na_width
    p = proj(c, c + pool_width); c += pool_width
    xq = proj(c, c + xa_width)
    qn = q * lax.rsqrt(_group_mean_sq(q, ones_na_ref, na_head_dim) + EPS) * qg_ref[...]
    q_out[...] = (qn * (na_head_dim ** -0.5)).astype(BF16)
    kn = k * lax.rsqrt(_group_mean_sq(k, ones_na_ref, na_head_dim) + EPS) * kg_ref[...]
    k_out[...] = kn.astype(BF16)
    v_out[...] = v.astype(BF16)
    p_out[...] = p
    xqn = xq * lax.rsqrt(_group_mean_sq(xq, ones_xa_ref, xa_head_dim) + EPS) * xg_ref[...]
    xq_out[...] = xqn.astype(BF16)


def _in_proj(x2, g, w_cols, qg, kg, xg, na_width, pool_width, xa_width):
    n, d = x2.shape
    tm = TOKEN_TILE
    na_hd = na_width // NA_HEADS
    xa_hd = xa_width // XA_HEADS
    assert na_hd in (4, 16, 64), "the folded attention scale must be a power of two"
    cols = w_cols.shape[1]
    const = lambda i: (0, 0)
    tile = lambda i: (i, 0)
    kern = functools.partial(_in_proj_kernel, na_width=na_width, pool_width=pool_width,
                             xa_width=xa_width, na_head_dim=na_hd, xa_head_dim=xa_hd)
    return pl.pallas_call(
        kern,
        grid=(n // tm,),
        in_specs=[
            pl.BlockSpec((tm, d), tile),
            pl.BlockSpec((1, d), const),
            pl.BlockSpec((d, cols), const),
            pl.BlockSpec((1, na_width), const),
            pl.BlockSpec((1, na_width), const),
            pl.BlockSpec((1, xa_width), const),
            pl.BlockSpec((na_width, na_width), const),
            pl.BlockSpec((xa_width, xa_width), const),
        ],
        out_specs=[
            pl.BlockSpec((tm, na_width), tile),
            pl.BlockSpec((tm, na_width), tile),
            pl.BlockSpec((tm, na_width), tile),
            pl.BlockSpec((tm, pool_width), tile),
            pl.BlockSpec((tm, xa_width), tile),
        ],
        out_shape=[
            jax.ShapeDtypeStruct((n, na_width), BF16),
            jax.ShapeDtypeStruct((n, na_width), BF16),
            jax.ShapeDtypeStruct((n, na_width), BF16),
            jax.ShapeDtypeStruct((n, pool_width), F32),
            jax.ShapeDtypeStruct((n, xa_width), BF16),
        ],
        compiler_params=pltpu.CompilerParams(dimension_semantics=("parallel",),
                                             vmem_limit_bytes=VMEM_LIMIT),
        name="in_proj",
    )(x2, g, w_cols, qg, kg, xg, _block_diag_ones(na_width, na_hd), _block_diag_ones(xa_width, xa_hd))


def _mem_kv_kernel(mem_ref, g_ref, w_ref, kg_ref, ones_ref, k_out, v_out, *, xa_width, xa_head_dim):
    mn = _rms_rows(mem_ref[0], g_ref[...]).astype(BF16)
    kv = jnp.dot(mn, w_ref[...], preferred_element_type=F32)
    k = kv[:, :xa_width]
    kn = k * lax.rsqrt(_group_mean_sq(k, ones_ref, xa_head_dim) + EPS) * kg_ref[...]
    k_out[0] = kn.astype(BF16)
    v_out[0] = kv[:, xa_width:].astype(BF16)


def _mem_kv(mem, g, w_kv, kg, xa_width):
    b, m, d = mem.shape
    xa_hd = xa_width // XA_HEADS
    const = lambda i: (0, 0)
    kern = functools.partial(_mem_kv_kernel, xa_width=xa_width, xa_head_dim=xa_hd)
    return pl.pallas_call(
        kern,
        grid=(b,),
        in_specs=[
            pl.BlockSpec((1, m, d), lambda i: (i, 0, 0)),
            pl.BlockSpec((1, d), const),
            pl.BlockSpec((d, 2 * xa_width), const),
            pl.BlockSpec((1, xa_width), const),
            pl.BlockSpec((xa_width, xa_width), const),
        ],
        out_specs=[pl.BlockSpec((1, m, xa_width), lambda i: (i, 0, 0))] * 2,
        out_shape=[jax.ShapeDtypeStruct((b, m, xa_width), BF16)] * 2,
        compiler_params=pltpu.CompilerParams(dimension_semantics=("parallel",),
                                             vmem_limit_bytes=VMEM_LIMIT),
        name="mem_kv",
    )(mem, g, w_kv, kg, _block_diag_ones(xa_width, xa_hd))


def _na_union_bias(rpb):
    rb, wr, w = NA_ROWS_PER_STEP, NA_WIN_ROWS, GRID_W
    union = wr + rb
    heads = rpb.shape[0]
    qc = np.arange(w)[:, None]
    kc = np.arange(w)[None, :]
    cs = np.clip(qc - NA_WIN_COLS // 2, 0, w - NA_WIN_COLS)
    col_ok = (kc >= cs) & (kc < cs + NA_WIN_COLS)
    pad = w - NA_WIN_COLS
    ext = jnp.pad(rpb.astype(F32), ((0, 0), (0, 0), (pad, pad)), mode="edge")
    by_q = jnp.stack([ext[:, :, w - 1 - c:2 * w - 1 - c] for c in range(w)], axis=2)
    by_q = jnp.where(col_ok[None, None], by_q, NEG_INF)
    patterns = ([(rq, 0) for rq in range(rb)], [(rb, rq) for rq in range(rb)],
                [(rb + rq, rb) for rq in range(rb)])
    pick = np.zeros((len(patterns), rb, union, 2 * wr), np.float32)
    pick[..., 2 * wr - 1] = 1.0
    for t, pattern in enumerate(patterns):
        for rq, (d, off) in enumerate(pattern):
            for i in range(wr):
                pick[t, rq, off + i, wr - 1 - d + i] = 1.0
                pick[t, rq, off + i, 2 * wr - 1] = 0.0
    by_q = jnp.concatenate([by_q, jnp.full((heads, 1, w, w), NEG_INF, F32)], axis=1)
    tab = jnp.einsum("trkd,hdqc->thrqkc", jnp.asarray(pick), by_q, precision=lax.Precision.HIGHEST)
    return tab.reshape(len(patterns), heads // 2, 2 * rb * w, union * w)


def _na_kernel(q_ref, k_ref, v_ref, bias_ref, o_ref, *, rows):
    blk = pl.program_id(1)
    rb = NA_ROWS_PER_STEP
    union = (NA_WIN_ROWS + rb) * GRID_W
    ks = jnp.clip(blk * rb - NA_WIN_ROWS // 2, 0, rows - NA_WIN_ROWS - rb)
    start = pl.multiple_of(ks * GRID_W, GRID_W)
    head_dim = q_ref.shape[-1] // NA_HEADS
    tokens = rb * GRID_W
    low = lax.broadcasted_iota(jnp.int32, (tokens, LANES), 1) < head_dim
    outs = []
    for t in range(q_ref.shape[-1] // LANES):
        cols = slice(t * LANES, (t + 1) * LANES)
        qt = q_ref[0, :, cols]
        kt = k_ref[0, pl.ds(start, union), cols]
        vt = v_ref[0, pl.ds(start, union), cols]
        zero = jnp.zeros_like(qt)
        q2 = jnp.concatenate([jnp.where(low, qt, zero), jnp.where(low, zero, qt)], axis=0)
        s = lax.dot_general(q2, kt, _NT, preferred_element_type=F32) + bias_ref[0, t]
        e = jnp.exp(s - jnp.max(s, axis=-1, keepdims=True))
        p = e / jnp.sum(e, axis=-1, keepdims=True)
        o2 = jnp.dot(p.astype(BF16), vt, preferred_element_type=F32)
        outs.append(jnp.where(low, o2[:tokens], o2[tokens:]))
    o_ref[0] = jnp.concatenate(outs, axis=-1).astype(BF16)


def _na_attention(q, k, v, bias_tab, batch, seq):
    width = q.shape[-1]
    rows = seq // GRID_W
    rb = NA_ROWS_PER_STEP
    assert LANES == 2 * (width // NA_HEADS), "two heads per lane tile"
    assert rb == NA_WIN_ROWS // 2 and rows % rb == 0 and rows >= NA_WIN_ROWS + 2 * rb
    nblk = rows // rb
    q3, k3, v3 = (a.reshape(batch, seq, width) for a in (q, k, v))
    tab = bias_tab

    def bias_idx(b, i):
        return (jnp.where(i == 0, 0, jnp.where(i == nblk - 1, 2, 1)), 0, 0, 0)

    full = lambda b, i: (b, 0, 0)
    once = pl.Buffered(1)
    out = pl.pallas_call(
        functools.partial(_na_kernel, rows=rows),
        grid=(batch, nblk),
        in_specs=[
            pl.BlockSpec((1, rb * GRID_W, width), lambda b, i: (b, i, 0)),
            pl.BlockSpec((1, seq, width), full, pipeline_mode=once),
            pl.BlockSpec((1, seq, width), full, pipeline_mode=once),
            pl.BlockSpec((1,) + tab.shape[1:], bias_idx, pipeline_mode=once),
        ],
        out_specs=pl.BlockSpec((1, rb * GRID_W, width), lambda b, i: (b, i, 0)),
        out_shape=jax.ShapeDtypeStruct((batch, seq, width), BF16),
        compiler_params=pltpu.CompilerParams(dimension_semantics=("parallel", "arbitrary"),
                                             vmem_limit_bytes=VMEM_LIMIT),
        name="na_attn",
    )(q3, k3, v3, tab)
    return out.reshape(batch * seq, width)


def _mix_kernel(x_ref, yna_ref, p_ref, pprev_ref, pnext_ref, xq_ref, km_ref, vm_ref,
                g_ref, wg_ref, gb_ref, pw_ref, ps_ref, wna_ref, wpool_ref, wxa_ref, wout_ref, fg_ref,
                h_out, hn_out, pbuf, *, seq, tiles_per_seq):
    tm, d = x_ref.shape
    ti = pl.program_id(0) % tiles_per_seq
    x = x_ref[...]
    xb = _rms_rows(x, g_ref[...]).astype(BF16)

    halo = POOL_HALO
    p = p_ref[...]
    pbuf[0:halo, :] = jnp.where(ti == 0, 0.0, pprev_ref[...])
    pbuf[halo:halo + tm, :] = p
    pbuf[halo + tm:2 * halo + tm, :] = jnp.where(ti == tiles_per_seq - 1, 0.0, pnext_ref[...])
    pos = ti * tm + lax.broadcasted_iota(jnp.int32, (tm, 1), 0)
    group = p.shape[-1] // len(POOL_SIZES)
    mixed = []
    for gi, w in enumerate(POOL_SIZES):
        cols = slice(gi * group, (gi + 1) * group)
        tot = pbuf[halo - w // 2:halo - w // 2 + tm, cols]
        for j in range(1 - w // 2, w // 2):
            tot = tot + pbuf[halo + j:halo + j + tm, cols]
        cnt = (jnp.minimum(pos + w // 2, seq) - jnp.maximum(pos - w // 2, 0)).astype(F32)
        pooled = tot / cnt - p[:, cols]
        mixed.append(jnp.dot(pooled.astype(BF16), pw_ref[gi], preferred_element_type=F32))
    y_pool = jnp.concatenate(mixed, axis=-1) * ps_ref[...]

    xq = xq_ref[...]
    xa_hd = xq.shape[-1] // XA_HEADS
    ys = []
    for h in range(XA_HEADS):
        cols = slice(h * xa_hd, (h + 1) * xa_hd)
        s = lax.dot_general(xq[:, cols], km_ref[0, :, cols], _NT, preferred_element_type=F32)
        s = s * (xa_hd ** -0.5)
        e = jnp.exp(s - jnp.max(s, axis=-1, keepdims=True))
        pr = e / jnp.sum(e, axis=-1, keepdims=True)
        ys.append(jnp.dot(pr.astype(BF16), vm_ref[0, :, cols], preferred_element_type=F32))
    y_xa = jnp.concatenate(ys, axis=-1)

    def gate(i):
        z = jnp.dot(xb, wg_ref[:, i * d:(i + 1) * d], preferred_element_type=F32)
        return jax.nn.sigmoid(z + gb_ref[:, i * d:(i + 1) * d])

    merged = (gate(0) * jnp.dot(yna_ref[...], wna_ref[...], preferred_element_type=F32)
              + gate(1) * jnp.dot(y_pool.astype(BF16), wpool_ref[...], preferred_element_type=F32)
              + gate(2) * jnp.dot(y_xa.astype(BF16), wxa_ref[...], preferred_element_type=F32))
    h = x + jnp.dot(merged.astype(BF16), wout_ref[...], preferred_element_type=F32)
    h_out[...] = h
    hn_out[...] = _rms_rows(h, fg_ref[...]).astype(BF16)


def _mix(x2, yna, p, xq, kmem, vmem, g, w_gate, gate_b, pool_w, pool_scale,
         w_na, w_pool, w_xa, w_out, ffn_g, seq):
    n, d = x2.shape
    tm = TOKEN_TILE
    assert seq % tm == 0 and tm % POOL_HALO == 0 and max(POOL_SIZES) // 2 == POOL_HALO
    tiles_per_seq = seq // tm
    halo_blocks = n // POOL_HALO
    per_tile = tm // POOL_HALO
    m = kmem.shape[1]
    na_w, pool_w_, xa_w = yna.shape[1], p.shape[1], xq.shape[1]
    const2 = lambda i: (0, 0)
    const3 = lambda i: (0, 0, 0)
    tile = lambda i: (i, 0)
    mem_idx = lambda i: (i // tiles_per_seq, 0, 0)
    kern = functools.partial(_mix_kernel, seq=seq, tiles_per_seq=tiles_per_seq)
    return pl.pallas_call(
        kern,
        grid=(n // tm,),
        in_specs=[
            pl.BlockSpec((tm, d), tile),
            pl.BlockSpec((tm, na_w), tile),
            pl.BlockSpec((tm, pool_w_), tile),
            pl.BlockSpec((POOL_HALO, pool_w_), lambda i: (jnp.maximum(i * per_tile - 1, 0), 0)),
            pl.BlockSpec((POOL_HALO, pool_w_), lambda i: (jnp.minimum((i + 1) * per_tile, halo_blocks - 1), 0)),
            pl.BlockSpec((tm, xa_w), tile),
            pl.BlockSpec((1, m, xa_w), mem_idx),
            pl.BlockSpec((1, m, xa_w), mem_idx),
            pl.BlockSpec((1, d), const2),
            pl.BlockSpec(w_gate.shape, const2),
            pl.BlockSpec(gate_b.shape, const2),
            pl.BlockSpec(pool_w.shape, const3),
            pl.BlockSpec((1, pool_w_), const2),
            pl.BlockSpec(w_na.shape, const2),
            pl.BlockSpec(w_pool.shape, const2),
            pl.BlockSpec(w_xa.shape, const2),
            pl.BlockSpec(w_out.shape, const2),
            pl.BlockSpec((1, d), const2),
        ],
        out_specs=[pl.BlockSpec((tm, d), tile), pl.BlockSpec((tm, d), tile)],
        out_shape=[jax.ShapeDtypeStruct((n, d), F32), jax.ShapeDtypeStruct((n, d), BF16)],
        scratch_shapes=[pltpu.VMEM((tm + 2 * POOL_HALO, pool_w_), F32)],
        compiler_params=pltpu.CompilerParams(dimension_semantics=("parallel",),
                                             vmem_limit_bytes=VMEM_LIMIT),
        name="mix",
    )(x2, yna, p, p, p, xq, kmem, vmem, g, w_gate, gate_b, pool_w, pool_scale,
      w_na, w_pool, w_xa, w_out, ffn_g)


def _gelu_exact(x):
    return 0.5 * x * (1.0 + lax.erf(x * (2.0 ** -0.5)))


def _candidate_pairs(depth):
    return [(i, j) for i in range(depth) for j in range(depth) if (i + 1) * (j + 1) <= depth]


def _descending_maxima(x, count):
    sub = 8
    tiles = [x[i * sub:(i + 1) * sub, :] for i in range(x.shape[0] // sub)]
    assert len(tiles) % 4 == 0
    stacks = []
    for g in range(0, len(tiles), 4):
        t = tiles[g:g + 4]
        for i, j in ((0, 1), (2, 3), (0, 2), (1, 3), (1, 2)):
            t[i], t[j] = jnp.maximum(t[i], t[j]), jnp.minimum(t[i], t[j])
        stacks.append(t)
    out = []
    for _ in range(count):
        m = jnp.max(functools.reduce(jnp.maximum, [t[0] for t in stacks]), axis=0, keepdims=True)
        out.append(m)
        for t in stacks:
            hit = t[0] >= m
            for lvl in range(3):
                t[lvl] = jnp.where(hit, t[lvl + 1], t[lvl])
            t[3] = jnp.where(hit, -jnp.inf, t[3])
    return out


def _peer_kernel(hn_ref, h_ref, wq_ref, sk_ref, u0_ref, un_ref, vt_ref, o_ref,
                 s2_ref, r2_ref, e2_ref, n_ref, e1_ref, top_ref, at0_ref, at1_ref, wa_ref, acc_ref):
    j = pl.program_id(1)
    tt = hn_ref.shape[0]
    keys = PEER_KEYS
    depth = PEER_TOPK + 1
    eb = un_ref.shape[0]
    groups = eb // keys

    def pre_activations(u_blk):
        return lax.dot_general(u_blk, hn_ref[...], _NT, preferred_element_type=F32).astype(BF16)

    @pl.when(j == 0)
    def _thresholds():
        lane_tiles = [slice(c * LANES, (c + 1) * LANES) for c in range(tt // LANES)]
        q = jnp.dot(hn_ref[...], wq_ref[...], preferred_element_type=F32).astype(BF16)
        for h in range(PEER_HEADS):
            for half in range(2):
                hp = 2 * h + half
                s = lax.dot_general(sk_ref[hp], q[:, hp * keys:(hp + 1) * keys], _NT,
                                    preferred_element_type=F32)
                if half == 0:
                    e1_ref[h] = s
                else:
                    s2_ref[h] = s
                for cols in lane_tiles:
                    x = s[:, cols]
                    if half == 0:
                        for r, m in enumerate(_descending_maxima(x, depth)):
                            top_ref[half, r, h:h + 1, cols] = m
                        continue
                    rank = jnp.full(x.shape, float(depth), F32)
                    for r in range(depth):
                        m = jnp.max(x, axis=0, keepdims=True)
                        top_ref[half, r, h:h + 1, cols] = m
                        hit = x >= m
                        rank = jnp.where(hit, float(r), rank)
                        x = jnp.where(hit, -jnp.inf, x)
                    r2_ref[h, :, cols] = rank.astype(BF16)
        for cols in lane_tiles:
            a = [top_ref[0, r, :, cols] for r in range(depth)]
            b = [top_ref[1, r, :, cols] for r in range(depth)]
            cands = [a[i] + b[k] for i, k in _candidate_pairs(depth)]
            best = []
            for r in range(depth):
                m = functools.reduce(jnp.maximum, cands)
                best.append(m)
                if r + 1 < depth:
                    cands = [jnp.where(c >= m, -jnp.inf, c) for c in cands]
            kth, nxt = best[PEER_TOPK - 1], best[PEER_TOPK]
            tau = jnp.where(nxt > -jnp.inf, 0.5 * (kth + nxt), kth)
            z = functools.reduce(jnp.add, [jnp.exp(c - best[0]) for c in best[:PEER_TOPK]])
            zinv = 1.0 / z
            for h in range(PEER_HEADS):
                s1 = e1_ref[h, :, cols]
                theta = tau[h:h + 1] - s1
                count = jnp.zeros(s1.shape, F32)
                for r in range(depth):
                    count = jnp.where(b[r][h:h + 1] >= theta, float(r + 1), count)
                n_ref[h, :, cols] = count
                e1_ref[h, :, cols] = jnp.exp(s1 - a[0][h:h + 1])
                e2 = jnp.exp(s2_ref[h, :, cols] - b[0][h:h + 1]) * zinv[h:h + 1]
                e2_ref[h, :, cols] = e2.astype(BF16)
        acc_ref[...] = jnp.zeros_like(acc_ref)
        at0_ref[...] = pre_activations(u0_ref[...])

    rb = 16
    half_rows = eb // 2

    def step(at_cur, at_next):
        at_next[...] = pre_activations(un_ref[...])
        base = pl.multiple_of(j * groups, groups)
        n_tile = [n_ref[h, pl.ds(base, groups), :] for h in range(PEER_HEADS)]
        e1_tile = [e1_ref[h, pl.ds(base, groups), :] for h in range(PEER_HEADS)]
        for l in range(groups):
            row = lambda tile: jnp.broadcast_to(tile[l:l + 1, :], (rb, tt)).astype(BF16)
            cnt = [row(n_tile[h]) for h in range(PEER_HEADS)]
            e1 = [row(e1_tile[h]) for h in range(PEER_HEADS)]
            for blk in range(keys // rb):
                rows = pl.ds(blk * rb, rb)
                dst = pl.ds(l * keys + blk * rb, rb)
                wt = None
                for h in range(PEER_HEADS):
                    sel = r2_ref[h, rows, :] < cnt[h]
                    w = jnp.where(sel, e2_ref[h, rows, :], jnp.zeros((), BF16)) * e1[h]
                    wt = w if wt is None else wt + w
                wa_ref[dst, :] = wt * _gelu_exact(at_cur[dst, :])
            if (l + 1) * keys == half_rows:
                acc_ref[...] += jnp.dot(vt_ref[0, :, :half_rows], wa_ref[:half_rows, :],
                                        preferred_element_type=F32)
        acc_ref[...] += jnp.dot(vt_ref[0, :, half_rows:], wa_ref[half_rows:, :],
                                preferred_element_type=F32)

    @pl.when(j % 2 == 0)
    def _even():
        step(at0_ref, at1_ref)

    @pl.when(j % 2 == 1)
    def _odd():
        step(at1_ref, at0_ref)

    @pl.when(j == pl.num_programs(1) - 1)
    def _finish():
        o_ref[...] = h_ref[...] + acc_ref[...].T


def _peer(hn, h, w_q, sub_keys, u, v):
    n, d = hn.shape
    tt = PEER_TOKEN_TILE
    eb = EXPERT_BLOCK
    experts = u.shape[0]
    assert experts == PEER_KEYS * PEER_KEYS and experts % eb == 0 and eb % PEER_KEYS == 0
    qcols = w_q.shape[1]
    assert qcols == PEER_HEADS * 2 * PEER_KEYS, "the query sub-dimension must equal the lane width"
    depth = PEER_TOPK + 1
    v_t = v.reshape(experts // eb, eb, d).transpose(0, 2, 1)
    return pl.pallas_call(
        _peer_kernel,
        grid=(n // tt, experts // eb),
        in_specs=[
            pl.BlockSpec((tt, d), lambda i, j: (i, 0)),
            pl.BlockSpec((tt, d), lambda i, j: (i, 0)),
            pl.BlockSpec((d, qcols), lambda i, j: (0, 0)),
            pl.BlockSpec(sub_keys.shape, lambda i, j: (0, 0, 0)),
            pl.BlockSpec((eb, d), lambda i, j: (0, 0)),
            pl.BlockSpec((eb, d), lambda i, j: (jnp.minimum(j + 1, experts // eb - 1), 0)),
            pl.BlockSpec((1, d, eb), lambda i, j: (j, 0, 0)),
        ],
        out_specs=pl.BlockSpec((tt, d), lambda i, j: (i, 0)),
        out_shape=jax.ShapeDtypeStruct((n, d), F32),
        scratch_shapes=[
            pltpu.VMEM((PEER_HEADS, PEER_KEYS, tt), F32),
            pltpu.VMEM((PEER_HEADS, PEER_KEYS, tt), BF16),
            pltpu.VMEM((PEER_HEADS, PEER_KEYS, tt), BF16),
            pltpu.VMEM((PEER_HEADS, PEER_KEYS, tt), F32),
            pltpu.VMEM((PEER_HEADS, PEER_KEYS, tt), F32),
            pltpu.VMEM((2, depth, PEER_HEADS, tt), F32),
            pltpu.VMEM((eb, tt), BF16),
            pltpu.VMEM((eb, tt), BF16),
            pltpu.VMEM((eb, tt), BF16),
            pltpu.VMEM((d, tt), F32),
        ],
        compiler_params=pltpu.CompilerParams(dimension_semantics=("parallel", "arbitrary"),
                                             vmem_limit_bytes=VMEM_LIMIT),
        name="peer",
    )(hn, h, w_q, sub_keys, u, u, v_t)


def kernel(x, mem, mix_norm_g, mem_norm_g, w_in, gate_b, w_mem_kv, na_q_g, na_k_g, na_rpb, pool_w, pool_scale, xa_q_g, xa_k_g, w_branch_na, w_branch_pool, w_branch_xa, w_out, ffn_norm_g, peer_w_q, peer_sub_keys, peer_u, peer_v):
    batch, seq, d = x.shape
    na_width = w_branch_na.shape[1]
    pool_width = w_branch_pool.shape[1]
    xa_width = w_branch_xa.shape[1]
    mix_cols = 3 * na_width + pool_width + xa_width
    n = batch * seq
    assert n % TOKEN_TILE == 0 and n % PEER_TOKEN_TILE == 0 and seq % GRID_W == 0
    row = lambda a: a.reshape(1, -1)
    h = x.reshape(n, d)
    for l in range(mix_norm_g.shape[0]):
        w_in_b = w_in[l].astype(BF16)
        q, k, v, p, xq = _in_proj(
            h, row(mix_norm_g[l]), w_in_b[:, :mix_cols],
            row(jnp.tile(na_q_g[l], NA_HEADS)), row(jnp.tile(na_k_g[l], NA_HEADS)),
            row(jnp.tile(xa_q_g[l], XA_HEADS)), na_width, pool_width, xa_width)
        kmem, vmem = _mem_kv(mem, row(mem_norm_g[l]), w_mem_kv[l].astype(BF16),
                             row(jnp.tile(xa_k_g[l], XA_HEADS)), xa_width)
        y_na = _na_attention(q, k, v, _na_union_bias(na_rpb[l]), batch, seq)
        h, hn = _mix(h, y_na, p, xq, kmem, vmem, row(mix_norm_g[l]), w_in_b[:, mix_cols:],
                     row(gate_b[l]), pool_w[l].astype(BF16), row(pool_scale[l]),
                     w_branch_na[l].astype(BF16), w_branch_pool[l].astype(BF16),
                     w_branch_xa[l].astype(BF16), w_out[l].astype(BF16), row(ffn_norm_g[l]), seq)
        sub_keys = peer_sub_keys[l].reshape(-1, *peer_sub_keys.shape[-2:]).astype(BF16)
        h = _peer(hn, h, peer_w_q[l].astype(BF16), sub_keys,
                  peer_u[l].astype(BF16), peer_v[l].astype(BF16))
    return h.reshape(batch, seq, d)
```

```python
import functools

import numpy as np
import jax
import jax.numpy as jnp
from jax import lax
from jax.experimental import pallas as pl
from jax.experimental.pallas import tpu as pltpu

GRID_W = 64
EPS = 1e-6
NEG_INF = -1e30

NA_HEADS = 8
NA_WIN_ROWS = 8
NA_WIN_COLS = 16
NA_ROWS_PER_STEP = 4
POOL_SIZES = (2, 4, 8, 16)
POOL_HALO = 8
XA_HEADS = 4
PEER_HEADS = 8
PEER_KEYS = 128
PEER_TOPK = 16

LANES = 128
TOKEN_TILE = 512
PEER_TOKEN_TILE = 512
EXPERT_BLOCK = 2048
VMEM_LIMIT = 56 * 1024 * 1024

F32 = jnp.float32
BF16 = jnp.bfloat16
_NT = (((1,), (1,)), ((), ()))


def _rms_rows(x, g):
    return x * lax.rsqrt(jnp.mean(x * x, axis=-1, keepdims=True) + EPS) * g


def _group_mean_sq(v, ones_ref, width):
    sq = (v * v).astype(BF16)
    return jnp.dot(sq, ones_ref[...], preferred_element_type=F32) * (1.0 / width)


def _block_diag_ones(channels, width):
    idx = np.arange(channels) // width
    return jnp.asarray(idx[:, None] == idx[None, :], dtype=BF16)


def _in_proj_kernel(x_ref, g_ref, w_ref, qg_ref, kg_ref, xg_ref, ones_na_ref, ones_xa_ref,
                    q_out, k_out, v_out, p_out, xq_out, *, na_width, pool_width, xa_width,
                    na_head_dim, xa_head_dim):
    xb = _rms_rows(x_ref[...], g_ref[...]).astype(BF16)

    def proj(c0, c1):
        return jnp.dot(xb, w_ref[:, c0:c1], preferred_element_type=F32)

    c = 0
    q = proj(c, c + na_width); c += na_width
    k = proj(c, c + na_width); c += na_width
    v = proj(c, c + na_width); c += na_width
    p = proj(c, c + pool_width); c += pool_width
    xq = proj(c, c + xa_width)
    qn = q * lax.rsqrt(_group_mean_sq(q, ones_na_ref, na_head_dim) + EPS) * qg_ref[...]
    q_out[...] = (qn * (na_head_dim ** -0.5)).astype(BF16)
    kn = k * lax.rsqrt(_group_mean_sq(k, ones_na_ref, na_head_dim) + EPS) * kg_ref[...]
    k_out[...] = kn.astype(BF16)
    v_out[...] = v.astype(BF16)
    p_out[...] = p
    xqn = xq * lax.rsqrt(_group_mean_sq(xq, ones_xa_ref, xa_head_dim) + EPS) * xg_ref[...]
    xq_out[...] = xqn.astype(BF16)


def _in_proj(x2, g, w_cols, qg, kg, xg, na_width, pool_width, xa_width):
    n, d = x2.shape
    tm = TOKEN_TILE
    na_hd = na_width // NA_HEADS
    xa_hd = xa_width // XA_HEADS
    assert na_hd in (4, 16, 64), "the folded attention scale must be a power of two"
    cols = w_cols.shape[1]
    const = lambda i: (0, 0)
    tile = lambda i: (i, 0)
    kern = functools.partial(_in_proj_kernel, na_width=na_width, pool_width=pool_width,
                             xa_width=xa_width, na_head_dim=na_hd, xa_head_dim=xa_hd)
    return pl.pallas_call(
        kern,
        grid=(n // tm,),
        in_specs=[
            pl.BlockSpec((tm, d), tile),
            pl.BlockSpec((1, d), const),
            pl.BlockSpec((d, cols), const),
            pl.BlockSpec((1, na_width), const),
            pl.BlockSpec((1, na_width), const),
            pl.BlockSpec((1, xa_width), const),
            pl.BlockSpec((na_width, na_width), const),
            pl.BlockSpec((xa_width, xa_width), const),
        ],
        out_specs=[
            pl.BlockSpec((tm, na_width), tile),
            pl.BlockSpec((tm, na_width), tile),
            pl.BlockSpec((tm, na_width), tile),
            pl.BlockSpec((tm, pool_width), tile),
            pl.BlockSpec((tm, xa_width), tile),
        ],
        out_shape=[
            jax.ShapeDtypeStruct((n, na_width), BF16),
            jax.ShapeDtypeStruct((n, na_width), BF16),
            jax.ShapeDtypeStruct((n, na_width), BF16),
            jax.ShapeDtypeStruct((n, pool_width), F32),
            jax.ShapeDtypeStruct((n, xa_width), BF16),
        ],
        compiler_params=pltpu.CompilerParams(dimension_semantics=("parallel",),
                                             vmem_limit_bytes=VMEM_LIMIT),
        name="in_proj",
    )(x2, g, w_cols, qg, kg, xg, _block_diag_ones(na_width, na_hd), _block_diag_ones(xa_width, xa_hd))


def _mem_kv_kernel(mem_ref, g_ref, w_ref, kg_ref, ones_ref, k_out, v_out, *, xa_width, xa_head_dim):
    mn = _rms_rows(mem_ref[0], g_ref[...]).astype(BF16)
    kv = jnp.dot(mn, w_ref[...], preferred_element_type=F32)
    k = kv[:, :xa_width]
    kn = k * lax.rsqrt(_group_mean_sq(k, ones_ref, xa_head_dim) + EPS) * kg_ref[...]
    k_out[0] = kn.astype(BF16)
    v_out[0] = kv[:, xa_width:].astype(BF16)


def _mem_kv(mem, g, w_kv, kg, xa_width):
    b, m, d = mem.shape
    xa_hd = xa_width // XA_HEADS
    const = lambda i: (0, 0)
    kern = functools.partial(_mem_kv_kernel, xa_width=xa_width, xa_head_dim=xa_hd)
    return pl.pallas_call(
        kern,
        grid=(b,),
        in_specs=[
            pl.BlockSpec((1, m, d), lambda i: (i, 0, 0)),
            pl.BlockSpec((1, d), const),
            pl.BlockSpec((d, 2 * xa_width), const),
            pl.BlockSpec((1, xa_width), const),
            pl.BlockSpec((xa_width, xa_width), const),
        ],
        out_specs=[pl.BlockSpec((1, m, xa_width), lambda i: (i, 0, 0))] * 2,
        out_shape=[jax.ShapeDtypeStruct((b, m, xa_width), BF16)] * 2,
        compiler_params=pltpu.CompilerParams(dimension_semantics=("parallel",),
                                             vmem_limit_bytes=VMEM_LIMIT),
        name="mem_kv",
    )(mem, g, w_kv, kg, _block_diag_ones(xa_width, xa_hd))


def _na_union_bias(rpb):
    rb, wr, w = NA_ROWS_PER_STEP, NA_WIN_ROWS, GRID_W
    union = wr + rb
    heads = rpb.shape[0]
    qc = np.arange(w)[:, None]
    kc = np.arange(w)[None, :]
    cs = np.clip(qc - NA_WIN_COLS // 2, 0, w - NA_WIN_COLS)
    col_ok = (kc >= cs) & (kc < cs + NA_WIN_COLS)
    pad = w - NA_WIN_COLS
    ext = jnp.pad(rpb.astype(F32), ((0, 0), (0, 0), (pad, pad)), mode="edge")
    by_q = jnp.stack([ext[:, :, w - 1 - c:2 * w - 1 - c] for c in range(w)], axis=2)
    by_q = jnp.where(col_ok[None, None], by_q, NEG_INF)
    patterns = ([(rq, 0) for rq in range(rb)], [(rb, rq) for rq in range(rb)],
                [(rb + rq, rb) for rq in range(rb)])
    pick = np.zeros((len(patterns), rb, union, 2 * wr), np.float32)
    pick[..., 2 * wr - 1] = 1.0
    for t, pattern in enumerate(patterns):
        for rq, (d, off) in enumerate(pattern):
            for i in range(wr):
                pick[t, rq, off + i, wr - 1 - d + i] = 1.0
                pick[t, rq, off + i, 2 * wr - 1] = 0.0
    by_q = jnp.concatenate([by_q, jnp.full((heads, 1, w, w), NEG_INF, F32)], axis=1)
    tab = jnp.einsum("trkd,hdqc->thrqkc", jnp.asarray(pick), by_q, precision=lax.Precision.HIGHEST)
    return tab.reshape(len(patterns), heads // 2, 2 * rb * w, union * w)


def _na_kernel(q_ref, k_ref, v_ref, bias_ref, o_ref, *, rows):
    blk = pl.program_id(1)
    rb = NA_ROWS_PER_STEP
    union = (NA_WIN_ROWS + rb) * GRID_W
    ks = jnp.clip(blk * rb - NA_WIN_ROWS // 2, 0, rows - NA_WIN_ROWS - rb)
    start = pl.multiple_of(ks * GRID_W, GRID_W)
    head_dim = q_ref.shape[-1] // NA_HEADS
    tokens = rb * GRID_W
    low = lax.broadcasted_iota(jnp.int32, (tokens, LANES), 1) < head_dim
    outs = []
    for t in range(q_ref.shape[-1] // LANES):
        cols = slice(t * LANES, (t + 1) * LANES)
        qt = q_ref[0, :, cols]
        kt = k_ref[0, pl.ds(start, union), cols]
        vt = v_ref[0, pl.ds(start, union), cols]
        zero = jnp.zeros_like(qt)
        q2 = jnp.concatenate([jnp.where(low, qt, zero), jnp.where(low, zero, qt)], axis=0)
        s = lax.dot_general(q2, kt, _NT, preferred_element_type=F32) + bias_ref[0, t]
        e = jnp.exp(s - jnp.max(s, axis=-1, keepdims=True))
        p = e / jnp.sum(e, axis=-1, keepdims=True)
        o2 = jnp.dot(p.astype(BF16), vt, preferred_element_type=F32)
        outs.append(jnp.where(low, o2[:tokens], o2[tokens:]))
    o_ref[0] = jnp.concatenate(outs, axis=-1).astype(BF16)


def _na_attention(q, k, v, bias_tab, batch, seq):
    width = q.shape[-1]
    rows = seq // GRID_W
    rb = NA_ROWS_PER_STEP
    assert LANES == 2 * (width // NA_HEADS), "two heads per lane tile"
    assert rb == NA_WIN_ROWS // 2 and rows % rb == 0 and rows >= NA_WIN_ROWS + 2 * rb
    nblk = rows // rb
    q3, k3, v3 = (a.reshape(batch, seq, width) for a in (q, k, v))
    tab = bias_tab

    def bias_idx(b, i):
        return (jnp.where(i == 0, 0, jnp.where(i == nblk - 1, 2, 1)), 0, 0, 0)

    full = lambda b, i: (b, 0, 0)
    once = pl.Buffered(1)
    out = pl.pallas_call(
        functools.partial(_na_kernel, rows=rows),
        grid=(batch, nblk),
        in_specs=[
            pl.BlockSpec((1, rb * GRID_W, width), lambda b, i: (b, i, 0)),
            pl.BlockSpec((1, seq, width), full, pipeline_mode=once),
            pl.BlockSpec((1, seq, width), full, pipeline_mode=once),
            pl.BlockSpec((1,) + tab.shape[1:], bias_idx, pipeline_mode=once),
        ],
        out_specs=pl.BlockSpec((1, rb * GRID_W, width), lambda b, i: (b, i, 0)),
        out_shape=jax.ShapeDtypeStruct((batch, seq, width), BF16),
        compiler_params=pltpu.CompilerParams(dimension_semantics=("parallel", "arbitrary"),
                                             vmem_limit_bytes=VMEM_LIMIT),
        name="na_attn",
    )(q3, k3, v3, tab)
    return out.reshape(batch * seq, width)


def _mix_kernel(x_ref, yna_ref, p_ref, pprev_ref, pnext_ref, xq_ref, mem_ref, mg_ref, wkv_ref, kg_ref,
                ones_ref, g_ref, wg_ref, gb_ref, pw_ref, ps_ref, wna_ref, wpool_ref, wxa_ref, wout_ref,
                fg_ref, h_out, hn_out, pbuf, km_s, vm_s, *, seq, tiles_per_seq):
    tm, d = x_ref.shape
    ti = pl.program_id(0) % tiles_per_seq

    @pl.when(ti == 0)
    def _memory_kv():
        xa_w = km_s.shape[-1]
        mn = _rms_rows(mem_ref[0], mg_ref[...]).astype(BF16)
        kv = jnp.dot(mn, wkv_ref[...], preferred_element_type=F32)
        k = kv[:, :xa_w]
        kn = k * lax.rsqrt(_group_mean_sq(k, ones_ref, xa_w // XA_HEADS) + EPS) * kg_ref[...]
        km_s[...] = kn.astype(BF16)
        vm_s[...] = kv[:, xa_w:].astype(BF16)

    x = x_ref[...]
    xb = _rms_rows(x, g_ref[...]).astype(BF16)

    halo = POOL_HALO
    p = p_ref[...]
    pbuf[0:halo, :] = jnp.where(ti == 0, 0.0, pprev_ref[...])
    pbuf[halo:halo + tm, :] = p
    pbuf[halo + tm:2 * halo + tm, :] = jnp.where(ti == tiles_per_seq - 1, 0.0, pnext_ref[...])
    pos = ti * tm + lax.broadcasted_iota(jnp.int32, (tm, 1), 0)
    group = p.shape[-1] // len(POOL_SIZES)
    mixed = []
    for gi, w in enumerate(POOL_SIZES):
        cols = slice(gi * group, (gi + 1) * group)
        tot = pbuf[halo - w // 2:halo - w // 2 + tm, cols]
        for j in range(1 - w // 2, w // 2):
            tot = tot + pbuf[halo + j:halo + j + tm, cols]
        cnt = (jnp.minimum(pos + w // 2, seq) - jnp.maximum(pos - w // 2, 0)).astype(F32)
        pooled = tot / cnt - p[:, cols]
        mixed.append(jnp.dot(pooled.astype(BF16), pw_ref[gi], preferred_element_type=F32))
    y_pool = jnp.concatenate(mixed, axis=-1) * ps_ref[...]

    xq = xq_ref[...]
    xa_hd = xq.shape[-1] // XA_HEADS
    ys = []
    for h in range(XA_HEADS):
        cols = slice(h * xa_hd, (h + 1) * xa_hd)
        s = lax.dot_general(xq[:, cols], km_s[:, cols], _NT, preferred_element_type=F32)
        s = s * (xa_hd ** -0.5)
        e = jnp.exp(s - jnp.max(s, axis=-1, keepdims=True))
        pr = e / jnp.sum(e, axis=-1, keepdims=True)
        ys.append(jnp.dot(pr.astype(BF16), vm_s[:, cols], preferred_element_type=F32))
    y_xa = jnp.concatenate(ys, axis=-1)

    def gate(i):
        z = jnp.dot(xb, wg_ref[:, i * d:(i + 1) * d], preferred_element_type=F32)
        return jax.nn.sigmoid(z + gb_ref[:, i * d:(i + 1) * d])

    merged = (gate(0) * jnp.dot(yna_ref[...], wna_ref[...], preferred_element_type=F32)
              + gate(1) * jnp.dot(y_pool.astype(BF16), wpool_ref[...], preferred_element_type=F32)
              + gate(2) * jnp.dot(y_xa.astype(BF16), wxa_ref[...], preferred_element_type=F32))
    h = x + jnp.dot(merged.astype(BF16), wout_ref[...], preferred_element_type=F32)
    h_out[...] = h
    hn_out[...] = _rms_rows(h, fg_ref[...]).astype(BF16)


def _mix(x2, yna, p, xq, mem, mem_g, w_kv, k_g, g, w_gate, gate_b, pool_w, pool_scale,
         w_na, w_pool, w_xa, w_out, ffn_g, seq):
    n, d = x2.shape
    tm = TOKEN_TILE
    assert seq % tm == 0 and tm % POOL_HALO == 0 and max(POOL_SIZES) // 2 == POOL_HALO
    tiles_per_seq = seq // tm
    halo_blocks = n // POOL_HALO
    per_tile = tm // POOL_HALO
    m = mem.shape[1]
    na_w, pool_w_, xa_w = yna.shape[1], p.shape[1], xq.shape[1]
    const2 = lambda i: (0, 0)
    const3 = lambda i: (0, 0, 0)
    tile = lambda i: (i, 0)
    mem_idx = lambda i: (i // tiles_per_seq, 0, 0)
    kern = functools.partial(_mix_kernel, seq=seq, tiles_per_seq=tiles_per_seq)
    return pl.pallas_call(
        kern,
        grid=(n // tm,),
        in_specs=[
            pl.BlockSpec((tm, d), tile),
            pl.BlockSpec((tm, na_w), tile),
            pl.BlockSpec((tm, pool_w_), tile),
            pl.BlockSpec((POOL_HALO, pool_w_), lambda i: (jnp.maximum(i * per_tile - 1, 0), 0)),
            pl.BlockSpec((POOL_HALO, pool_w_), lambda i: (jnp.minimum((i + 1) * per_tile, halo_blocks - 1), 0)),
            pl.BlockSpec((tm, xa_w), tile),
            pl.BlockSpec((1, m, d), mem_idx),
            pl.BlockSpec((1, d), const2),
            pl.BlockSpec(w_kv.shape, const2),
            pl.BlockSpec((1, xa_w), const2),
            pl.BlockSpec((xa_w, xa_w), const2),
            pl.BlockSpec((1, d), const2),
            pl.BlockSpec(w_gate.shape, const2),
            pl.BlockSpec(gate_b.shape, const2),
            pl.BlockSpec(pool_w.shape, const3),
            pl.BlockSpec((1, pool_w_), const2),
            pl.BlockSpec(w_na.shape, const2),
            pl.BlockSpec(w_pool.shape, const2),
            pl.BlockSpec(w_xa.shape, const2),
            pl.BlockSpec(w_out.shape, const2),
            pl.BlockSpec((1, d), const2),
        ],
        out_specs=[pl.BlockSpec((tm, d), tile), pl.BlockSpec((tm, d), tile)],
        out_shape=[jax.ShapeDtypeStruct((n, d), F32), jax.ShapeDtypeStruct((n, d), BF16)],
        scratch_shapes=[pltpu.VMEM((tm + 2 * POOL_HALO, pool_w_), F32),
                        pltpu.VMEM((m, xa_w), BF16), pltpu.VMEM((m, xa_w), BF16)],
        compiler_params=pltpu.CompilerParams(dimension_semantics=("arbitrary",),
                                             vmem_limit_bytes=VMEM_LIMIT),
        name="mix",
    )(x2, yna, p, p, p, xq, mem, mem_g, w_kv, k_g, _block_diag_ones(xa_w, xa_w // XA_HEADS),
      g, w_gate, gate_b, pool_w, pool_scale,
      w_na, w_pool, w_xa, w_out, ffn_g)


def _gelu_exact(x):
    return 0.5 * x * (1.0 + lax.erf(x * (2.0 ** -0.5)))


def _candidate_pairs(depth):
    return [(i, j) for i in range(depth) for j in range(depth) if (i + 1) * (j + 1) <= depth]


def _descending_maxima(x, count):
    sub = 8
    tiles = [x[i * sub:(i + 1) * sub, :] for i in range(x.shape[0] // sub)]
    assert len(tiles) % 4 == 0
    stacks = []
    for g in range(0, len(tiles), 4):
        t = tiles[g:g + 4]
        for i, j in ((0, 1), (2, 3), (0, 2), (1, 3), (1, 2)):
            t[i], t[j] = jnp.maximum(t[i], t[j]), jnp.minimum(t[i], t[j])
        stacks.append(t)
    out = []
    for _ in range(count):
        m = jnp.max(functools.reduce(jnp.maximum, [t[0] for t in stacks]), axis=0, keepdims=True)
        out.append(m)
        for t in stacks:
            hit = t[0] >= m
            for lvl in range(3):
                t[lvl] = jnp.where(hit, t[lvl + 1], t[lvl])
            t[3] = jnp.where(hit, -jnp.inf, t[3])
    return out


def _peer_kernel(hn_ref, h_ref, wq_ref, sk_ref, u0_ref, un_ref, vt_ref, o_ref,
                 s2_ref, r2_ref, e2_ref, n_ref, e1_ref, top_ref, at0_ref, at1_ref, wa_ref, acc_ref):
    j = pl.program_id(1)
    tt = hn_ref.shape[0]
    keys = PEER_KEYS
    depth = PEER_TOPK + 1
    eb = un_ref.shape[0]
    groups = eb // keys

    def pre_activations(u_blk):
        return lax.dot_general(u_blk, hn_ref[...], _NT, preferred_element_type=F32).astype(BF16)

    @pl.when(j == 0)
    def _thresholds():
        lane_tiles = [slice(c * LANES, (c + 1) * LANES) for c in range(tt // LANES)]
        q = jnp.dot(hn_ref[...], wq_ref[...], preferred_element_type=F32).astype(BF16)
        for h in range(PEER_HEADS):
            for half in range(2):
                hp = 2 * h + half
                s = lax.dot_general(sk_ref[hp], q[:, hp * keys:(hp + 1) * keys], _NT,
                                    preferred_element_type=F32)
                if half == 0:
                    e1_ref[h] = s
                else:
                    s2_ref[h] = s
                for cols in lane_tiles:
                    x = s[:, cols]
                    if half == 0:
                        for r, m in enumerate(_descending_maxima(x, depth)):
                            top_ref[half, r, h:h + 1, cols] = m
                        continue
                    rank = jnp.full(x.shape, float(depth), F32)
                    for r in range(depth):
                        m = jnp.max(x, axis=0, keepdims=True)
                        top_ref[half, r, h:h + 1, cols] = m
                        hit = x >= m
                        rank = jnp.where(hit, float(r), rank)
                        x = jnp.where(hit, -jnp.inf, x)
                    r2_ref[h, :, cols] = rank.astype(BF16)
        for cols in lane_tiles:
            a = [top_ref[0, r, :, cols] for r in range(depth)]
            b = [top_ref[1, r, :, cols] for r in range(depth)]
            cands = [a[i] + b[k] for i, k in _candidate_pairs(depth)]
            best = []
            for r in range(depth):
                m = functools.reduce(jnp.maximum, cands)
                best.append(m)
                if r + 1 < depth:
                    cands = [jnp.where(c >= m, -jnp.inf, c) for c in cands]
            kth, nxt = best[PEER_TOPK - 1], best[PEER_TOPK]
            tau = jnp.where(nxt > -jnp.inf, 0.5 * (kth + nxt), kth)
            z = functools.reduce(jnp.add, [jnp.exp(c - best[0]) for c in best[:PEER_TOPK]])
            zinv = 1.0 / z
            for h in range(PEER_HEADS):
                s1 = e1_ref[h, :, cols]
                theta = tau[h:h + 1] - s1
                count = jnp.zeros(s1.shape, F32)
                for r in range(depth):
                    count = jnp.where(b[r][h:h + 1] >= theta, float(r + 1), count)
                n_ref[h, :, cols] = count
                e1_ref[h, :, cols] = jnp.exp(s1 - a[0][h:h + 1])
                e2 = jnp.exp(s2_ref[h, :, cols] - b[0][h:h + 1]) * zinv[h:h + 1]
                e2_ref[h, :, cols] = e2.astype(BF16)
        acc_ref[...] = jnp.zeros_like(acc_ref)
        at0_ref[...] = pre_activations(u0_ref[...])

    rb = 16

    def step(at_cur, at_next):
        at_next[...] = pre_activations(un_ref[...])
        sub = 8
        base = pl.multiple_of(j * groups, sub)
        for l in range(groups):
            if l % sub == 0:
                start = base + l
                n_tile = [n_ref[h, pl.ds(start, sub), :] for h in range(PEER_HEADS)]
                e1_tile = [e1_ref[h, pl.ds(start, sub), :] for h in range(PEER_HEADS)]
            row = lambda tile: jnp.broadcast_to(tile[l % sub:l % sub + 1, :], (rb, tt)).astype(BF16)
            cnt = [row(n_tile[h]) for h in range(PEER_HEADS)]
            e1 = [row(e1_tile[h]) for h in range(PEER_HEADS)]
            for blk in range(keys // rb):
                rows = pl.ds(blk * rb, rb)
                dst = pl.ds(l * keys + blk * rb, rb)
                wt = None
                for h in range(PEER_HEADS):
                    sel = r2_ref[h, rows, :] < cnt[h]
                    w = jnp.where(sel, e2_ref[h, rows, :], jnp.zeros((), BF16)) * e1[h]
                    wt = w if wt is None else wt + w
                wa_ref[dst, :] = wt * _gelu_exact(at_cur[dst, :])
        acc_ref[...] += jnp.dot(vt_ref[0], wa_ref[...], preferred_element_type=F32)

    @pl.when(j % 2 == 0)
    def _even():
        step(at0_ref, at1_ref)

    @pl.when(j % 2 == 1)
    def _odd():
        step(at1_ref, at0_ref)

    @pl.when(j == pl.num_programs(1) - 1)
    def _finish():
        o_ref[...] = h_ref[...] + acc_ref[...].T


def _peer(hn, h, w_q, sub_keys, u, v):
    n, d = hn.shape
    tt = PEER_TOKEN_TILE
    eb = EXPERT_BLOCK
    experts = u.shape[0]
    assert experts == PEER_KEYS * PEER_KEYS and experts % eb == 0 and eb % PEER_KEYS == 0
    assert (eb // PEER_KEYS) % 8 == 0, "a step's i1 rows must be whole f32 sublane tiles of the per-i1 factors"
    qcols = w_q.shape[1]
    assert qcols == PEER_HEADS * 2 * PEER_KEYS, "the query sub-dimension must equal the lane width"
    depth = PEER_TOPK + 1
    v_t = v.reshape(experts // eb, eb, d).transpose(0, 2, 1)
    once = pl.Buffered(1)
    return pl.pallas_call(
        _peer_kernel,
        grid=(n // tt, experts // eb),
        in_specs=[
            pl.BlockSpec((tt, d), lambda i, j: (i, 0)),
            pl.BlockSpec((tt, d), lambda i, j: (i, 0)),
            pl.BlockSpec((d, qcols), lambda i, j: (0, 0), pipeline_mode=once),
            pl.BlockSpec(sub_keys.shape, lambda i, j: (0, 0, 0), pipeline_mode=once),
            pl.BlockSpec((eb, d), lambda i, j: (0, 0), pipeline_mode=once),
            pl.BlockSpec((eb, d), lambda i, j: (jnp.minimum(j + 1, experts // eb - 1), 0)),
            pl.BlockSpec((1, d, eb), lambda i, j: (j, 0, 0)),
        ],
        out_specs=pl.BlockSpec((tt, d), lambda i, j: (i, 0)),
        out_shape=jax.ShapeDtypeStruct((n, d), F32),
        scratch_shapes=[
            pltpu.VMEM((PEER_HEADS, PEER_KEYS, tt), F32),
            pltpu.VMEM((PEER_HEADS, PEER_KEYS, tt), BF16),
            pltpu.VMEM((PEER_HEADS, PEER_KEYS, tt), BF16),
            pltpu.VMEM((PEER_HEADS, PEER_KEYS, tt), F32),
            pltpu.VMEM((PEER_HEADS, PEER_KEYS, tt), F32),
            pltpu.VMEM((2, depth, PEER_HEADS, tt), F32),
            pltpu.VMEM((eb, tt), BF16),
            pltpu.VMEM((eb, tt), BF16),
            pltpu.VMEM((eb, tt), BF16),
            pltpu.VMEM((d, tt), F32),
        ],
        compiler_params=pltpu.CompilerParams(dimension_semantics=("parallel", "arbitrary"),
                                             vmem_limit_bytes=VMEM_LIMIT),
        name="peer",
    )(hn, h, w_q, sub_keys, u, u, v_t)


def kernel(x, mem, mix_norm_g, mem_norm_g, w_in, gate_b, w_mem_kv, na_q_g, na_k_g, na_rpb, pool_w, pool_scale, xa_q_g, xa_k_g, w_branch_na, w_branch_pool, w_branch_xa, w_out, ffn_norm_g, peer_w_q, peer_sub_keys, peer_u, peer_v):
    batch, seq, d = x.shape
    na_width = w_branch_na.shape[1]
    pool_width = w_branch_pool.shape[1]
    xa_width = w_branch_xa.shape[1]
    mix_cols = 3 * na_width + pool_width + xa_width
    n = batch * seq
    assert n % TOKEN_TILE == 0 and n % PEER_TOKEN_TILE == 0 and seq % GRID_W == 0
    row = lambda a: a.reshape(1, -1)
    h = x.reshape(n, d)
    for l in range(mix_norm_g.shape[0]):
        w_in_b = w_in[l].astype(BF16)
        q, k, v, p, xq = _in_proj(
            h, row(mix_norm_g[l]), w_in_b[:, :mix_cols],
            row(jnp.tile(na_q_g[l], NA_HEADS)), row(jnp.tile(na_k_g[l], NA_HEADS)),
            row(jnp.tile(xa_q_g[l], XA_HEADS)), na_width, pool_width, xa_width)
        y_na = _na_attention(q, k, v, _na_union_bias(na_rpb[l]), batch, seq)
        h, hn = _mix(h, y_na, p, xq, mem, row(mem_norm_g[l]), w_mem_kv[l].astype(BF16),
                     row(jnp.tile(xa_k_g[l], XA_HEADS)), row(mix_norm_g[l]), w_in_b[:, mix_cols:],
                     row(gate_b[l]), pool_w[l].astype(BF16), row(pool_scale[l]),
                     w_branch_na[l].astype(BF16), w_branch_pool[l].astype(BF16),
                     w_branch_xa[l].astype(BF16), w_out[l].astype(BF16), row(ffn_norm_g[l]), seq)
        sub_keys = peer_sub_keys[l].reshape(-1, *peer_sub_keys.shape[-2:]).astype(BF16)
        h = _peer(hn, h, peer_w_q[l].astype(BF16), sub_keys,
                  peer_u[l].astype(BF16), peer_v[l].astype(BF16))
    return h.reshape(batch, seq, d)
```
